```python
import math
import jax, jax.numpy as jnp
from jax import lax
import numpy as np

D_MODEL = 1024
BATCH = 8
SEQ = 4096
DEPTH = 2

CTX_LEN = 256
GRID_W = 64

MIX_W = D_MODEL
N_GROUPS = 4
GROUP_W = MIX_W // N_GROUPS
IN_W = 8 * GROUP_W

NA_HEADS = 4
NA_HEAD_DIM = GROUP_W // NA_HEADS
NA_ROWS = 8
NA_COLS = 16

CONV_K = 3

FOURIER_HEADS = 4
FOURIER_DIM = GROUP_W // FOURIER_HEADS

S5_CH = 16
S5_GROUPS = GROUP_W // S5_CH
S5_STATE = 64

PEER_HEADS = 8
PEER_KEYS = 128
PEER_EXPERTS = PEER_KEYS * PEER_KEYS
PEER_TOPK = 16
PEER_QDIM = 256
PEER_HALF = PEER_QDIM // 2
PEER_CHUNK = 128

N_MOD = 6
EPS = 1e-6
NEG = -1e30

kernel_name = "hybrid_natten_conv_fnet_s5_peer_dit"


def rmsnorm(x, g):
    xf = x.astype(jnp.float32)
    y = xf * lax.rsqrt(jnp.mean(xf * xf, axis=-1, keepdims=True) + EPS)
    return y.astype(x.dtype) * g


def split_heads(t, n):
    b, l, _ = t.shape
    return t.reshape(b, l, n, -1).transpose(0, 2, 1, 3)


def neighbourhood_attention(q, k, v, q_c, k_c, v_c, rpb):
    b, s, _ = q.shape
    rows = s // GRID_W
    wr = min(NA_ROWS, rows)
    wc = min(NA_COLS, GRID_W)
    scale = NA_HEAD_DIM ** -0.5
    qg = split_heads(q, NA_HEADS).reshape(b, NA_HEADS, rows, GRID_W, NA_HEAD_DIM)
    kg = split_heads(k, NA_HEADS).reshape(b, NA_HEADS, rows, GRID_W, NA_HEAD_DIM)
    vg = split_heads(v, NA_HEADS).reshape(b, NA_HEADS, rows, GRID_W, NA_HEAD_DIM)
    kh_c = split_heads(k_c, NA_HEADS)
    vh_c = split_heads(v_c, NA_HEADS)

    r = jnp.arange(rows)
    col = jnp.arange(GRID_W)
    row_start = jnp.clip(r - wr // 2, 0, rows - wr)
    key_rows = row_start[:, None] + jnp.arange(wr)[None, :]
    col_start = jnp.clip(col - wc // 2, 0, GRID_W - wc)
    nk = wr * GRID_W
    k_blk = kg[:, :, key_rows].reshape(b, NA_HEADS, rows, nk, NA_HEAD_DIM)
    v_blk = vg[:, :, key_rows].reshape(b, NA_HEADS, rows, nk, NA_HEAD_DIM)
    key_col = jnp.broadcast_to(col[None, :], (wr, GRID_W)).reshape(nk)
    col_ok = (key_col[None, :] >= col_start[:, None]) & (key_col[None, :] < col_start[:, None] + wc)
    d_row = jnp.repeat(key_rows, GRID_W, axis=1) - r[:, None]
    d_col = jnp.clip(key_col[None, :] - col[:, None], -(NA_COLS - 1), NA_COLS - 1)
    bias = rpb[:, (d_row + NA_ROWS - 1)[:, None, :], (d_col + NA_COLS - 1)[None, :, :]]

    s_lat = jnp.einsum('bhrqd,bhrkd->bhrqk', qg, k_blk).astype(jnp.float32) * scale + bias.astype(jnp.float32)
    s_lat = jnp.where(col_ok, s_lat, NEG)
    s_ctx = jnp.einsum('bhrqd,bhcd->bhrqc', qg, kh_c).astype(jnp.float32) * scale
    p = jax.nn.softmax(jnp.concatenate([s_lat, s_ctx], axis=-1), axis=-1).astype(q.dtype)
    o = (jnp.einsum('bhrqk,bhrkd->bhrqd', p[..., :nk], v_blk)
         + jnp.einsum('bhrqc,bhcd->bhrqd', p[..., nk:], vh_c))
    o = o.transpose(0, 2, 3, 1, 4).reshape(b, s, GROUP_W)

    o_c = None
    if q_c is not None:
        lc = q_c.shape[1]
        qh_c = split_heads(q_c, NA_HEADS)
        s_cc = jnp.einsum('bhqd,bhkd->bhqk', qh_c, kh_c).astype(jnp.float32) * scale
        p_cc = jax.nn.softmax(s_cc, axis=-1).astype(q_c.dtype)
        o_c = jnp.einsum('bhqk,bhkd->bhqd', p_cc, vh_c).transpose(0, 2, 1, 3).reshape(b, lc, GROUP_W)
    return o, o_c


def short_gated_conv(gate_b, gate_c, xin, conv_w):
    z = gate_c * xin
    z = lax.conv_general_dilated(z, conv_w[:, None, :], window_strides=(1,),
                                 padding=((CONV_K // 2, CONV_K // 2),),
                                 dimension_numbers=('NWC', 'WIO', 'NWC'),
                                 feature_group_count=GROUP_W)
    return gate_b * z


def fourier_mix(z):
    b, l, _ = z.shape
    zz = z.astype(jnp.float32).reshape(b, l, FOURIER_HEADS, FOURIER_DIM)
    f = jnp.fft.fft2(zz, axes=(1, 3), norm='ortho').real
    return f.reshape(b, l, GROUP_W).astype(z.dtype)


def _linear_combine(e1, e2):
    a1, b1 = e1
    a2, b2 = e2
    return a1 * a2, a2 * b1 + b2


def s5_bidirectional(u, u_c, a_re, a_im, b_re, b_im, c_re, c_im, log_dt, d_skip, w_glu, ctx_out):
    f32 = jnp.float32
    A = lax.complex(a_re.astype(f32), a_im.astype(f32))
    dt = jnp.exp(log_dt.astype(f32))[..., None]
    a_bar = jnp.exp(A * dt)
    b_bar = ((a_bar - 1.0) / A)[..., None] * lax.complex(b_re.astype(f32), b_im.astype(f32))
    c_mat = lax.complex(c_re.astype(f32), c_im.astype(f32))

    def drive(t, dr):
        bz, l, _ = t.shape
        tg = t.astype(f32).reshape(bz, l, S5_GROUPS, S5_CH).astype(jnp.complex64)
        return jnp.einsum('blgh,gph->lbgp', tg, b_bar[dr])

    def scan(bseq, dr, reverse):
        a = jnp.broadcast_to(a_bar[dr][None, None], (bseq.shape[0], 1) + a_bar.shape[1:])
        _, h = lax.associative_scan(_linear_combine, (a, bseq), reverse=reverse)
        return h

    def readout(h, dr):
        return jnp.einsum('lbgp,ghp->blgh', h, c_mat[dr]).real

    def finish(t, y):
        bz, l, _ = t.shape
        y = y.reshape(bz, l, GROUP_W) + d_skip.astype(f32) * t.astype(f32)
        g = jax.nn.gelu(y)
        return (g * jax.nn.sigmoid(g @ w_glu.astype(f32))).astype(t.dtype)

    hf_c = scan(drive(u_c, 0), 0, False)
    hb_c = scan(drive(u_c, 1), 1, True)
    bf = drive(u, 0).at[0].add(a_bar[0] * hf_c[-1])
    bb = drive(u, 1).at[-1].add(a_bar[1] * hb_c[0])
    y_x = finish(u, readout(scan(bf, 0, False), 0) + readout(scan(bb, 1, True), 1))
    y_c = finish(u_c, readout(hf_c, 0) + readout(hb_c, 1)) if ctx_out else None
    return y_x, y_c


def peer_ffn(h, wq, keys, u, v):
    bz, l, d = h.shape
    t = bz * l
    ht = h.reshape(t, d)
    q = (ht @ wq).reshape(t, PEER_HEADS, 2, PEER_HALF)
    s = jnp.einsum('thzd,hznd->thzn', q, keys).astype(jnp.float32)
    s1, i1 = lax.top_k(s[:, :, 0], PEER_TOPK)
    s2, i2 = lax.top_k(s[:, :, 1], PEER_TOPK)
    cand = (s1[..., :, None] + s2[..., None, :]).reshape(t, PEER_HEADS, PEER_TOPK * PEER_TOPK)
    cidx = (i1[..., :, None] * PEER_KEYS + i2[..., None, :]).reshape(t, PEER_HEADS, PEER_TOPK * PEER_TOPK)
    top, pos = lax.top_k(cand, PEER_TOPK)
    eidx = jnp.take_along_axis(cidx, pos, axis=-1)
    gate = jax.nn.softmax(top, axis=-1)
    n_chunks = t // PEER_CHUNK

    def chunk_fn(args):
        xc, ic, gc = args
        act = jax.nn.gelu(jnp.einsum('td,thkd->thk', xc, u[ic]).astype(jnp.float32)) * gc
        return jnp.einsum('thk,thkd->td', act.astype(xc.dtype), v[ic])

    out = lax.map(chunk_fn, (ht.reshape(n_chunks, PEER_CHUNK, d),
                             eidx.reshape(n_chunks, PEER_CHUNK, PEER_HEADS, PEER_TOPK),
                             gate.reshape(n_chunks, PEER_CHUNK, PEER_HEADS, PEER_TOPK)))
    return out.reshape(bz, l, d).astype(h.dtype)


def merge_groups(outs, grp_g, w_out):
    y = rmsnorm(jnp.stack(outs, axis=-2), grp_g)
    return y.reshape(y.shape[:-2] + (MIX_W,)) @ w_out


def hybrid_layer(x, ctx, c, c_ctx, ada_w, ada_b, norm1_g, norm2_g, w_in, na_rpb, conv_w,
                 s5_a_re, s5_a_im, s5_b_re, s5_b_im, s5_c_re, s5_c_im, s5_log_dt, s5_d, s5_w_glu,
                 grp_g, w_out, peer_wq, peer_keys, peer_u, peer_v, update_ctx):
    G = GROUP_W
    mod_x = (jax.nn.silu(c) @ ada_w + ada_b)[:, None, :]
    mod_c = (jax.nn.silu(c_ctx) @ ada_w + ada_b)[None, None, :]
    sh1, sc1, g1, sh2, sc2, g2 = jnp.split(mod_x, N_MOD, axis=-1)
    csh1, csc1, cg1, csh2, csc2, cg2 = jnp.split(mod_c, N_MOD, axis=-1)

    hx = rmsnorm(x, norm1_g) * (1.0 + sc1) + sh1
    hc = rmsnorm(ctx, norm1_g) * (1.0 + csc1) + csh1
    q, k, v, cb, cc, cx, fz, su = jnp.split(hx @ w_in, 8, axis=-1)
    if update_ctx:
        q_c, k_c, v_c, cb_c, cc_c, cx_c, fz_c, su_c = jnp.split(hc @ w_in, 8, axis=-1)
    else:
        q_c = None
        k_c = hc @ w_in[:, 1 * G:2 * G]
        v_c = hc @ w_in[:, 2 * G:3 * G]
        su_c = hc @ w_in[:, 7 * G:8 * G]

    att_x, att_c = neighbourhood_attention(q, k, v, q_c, k_c, v_c, na_rpb)
    conv_x = short_gated_conv(cb, cc, cx, conv_w)
    four_x = fourier_mix(fz)
    ssm_x, ssm_c = s5_bidirectional(su, su_c, s5_a_re, s5_a_im, s5_b_re, s5_b_im, s5_c_re, s5_c_im,
                                    s5_log_dt, s5_d, s5_w_glu, update_ctx)

    x = x + g1 * merge_groups([att_x, conv_x, four_x, ssm_x], grp_g, w_out)
    x = x + g2 * peer_ffn(rmsnorm(x, norm2_g) * (1.0 + sc2) + sh2, peer_wq, peer_keys, peer_u, peer_v)

    if update_ctx:
        conv_c = short_gated_conv(cb_c, cc_c, cx_c, conv_w)
        four_c = fourier_mix(fz_c)
        ctx = ctx + cg1 * merge_groups([att_c, conv_c, four_c, ssm_c], grp_g, w_out)
        ctx = ctx + cg2 * peer_ffn(rmsnorm(ctx, norm2_g) * (1.0 + csc2) + csh2,
                                   peer_wq, peer_keys, peer_u, peer_v)
    return x, ctx


def setup_inputs(seed: int = 0) -> dict:
    key = jax.random.key(seed)
    ks = iter(jax.random.split(key, 40))
    f32 = jnp.float32

    def nrm(shape, s):
        return jax.random.normal(next(ks), shape, f32) * s

    D = D_MODEL
    n_idx = jnp.arange(S5_STATE, dtype=f32)
    return {
        "x": nrm((BATCH, SEQ, D), 1.0),
        "c": nrm((BATCH, D), 1.0),
        "ctx": nrm((BATCH, CTX_LEN, D), 1.0),
        "c_ctx": nrm((D,), 1.0),
        "ada_w": nrm((DEPTH, D, N_MOD * D), 0.5 * D ** -0.5),
        "ada_b": nrm((DEPTH, N_MOD * D), 0.02),
        "norm1_g": 1.0 + nrm((DEPTH, D), 0.02),
        "norm2_g": 1.0 + nrm((DEPTH, D), 0.02),
        "w_in": nrm((DEPTH, D, IN_W), D ** -0.5),
        "na_rpb": nrm((DEPTH, NA_HEADS, 2 * NA_ROWS - 1, 2 * NA_COLS - 1), 0.02),
        "conv_w": nrm((DEPTH, CONV_K, GROUP_W), CONV_K ** -0.5),
        "s5_a_re": -0.5 + nrm((DEPTH, 2, S5_GROUPS, S5_STATE), 0.01),
        "s5_a_im": math.pi * n_idx + nrm((DEPTH, 2, S5_GROUPS, S5_STATE), 0.01),
        "s5_b_re": nrm((DEPTH, 2, S5_GROUPS, S5_STATE, S5_CH), (2.0 * S5_CH) ** -0.5),
        "s5_b_im": nrm((DEPTH, 2, S5_GROUPS, S5_STATE, S5_CH), (2.0 * S5_CH) ** -0.5),
        "s5_c_re": nrm((DEPTH, 2, S5_GROUPS, S5_CH, S5_STATE), (2.0 * S5_STATE) ** -0.5),
        "s5_c_im": nrm((DEPTH, 2, S5_GROUPS, S5_CH, S5_STATE), (2.0 * S5_STATE) ** -0.5),
        "s5_log_dt": jax.random.uniform(next(ks), (DEPTH, 2, S5_GROUPS), f32,
                                        minval=math.log(1e-3), maxval=math.log(1e-1)),
        "s5_d": nrm((DEPTH, GROUP_W), 1.0),
        "s5_w_glu": nrm((DEPTH, GROUP_W, GROUP_W), GROUP_W ** -0.5),
        "grp_g": 1.0 + nrm((DEPTH, N_GROUPS, GROUP_W), 0.02),
        "w_out": nrm((DEPTH, MIX_W, D), MIX_W ** -0.5),
        "peer_wq": nrm((DEPTH, D, PEER_HEADS * PEER_QDIM), D ** -0.5),
        "peer_keys": nrm((DEPTH, PEER_HEADS, 2, PEER_KEYS, PEER_HALF), PEER_HALF ** -0.5),
        "peer_u": nrm((DEPTH, PEER_EXPERTS, D), D ** -0.5),
        "peer_v": nrm((DEPTH, PEER_EXPERTS, D), 1.0),
        "final_g": 1.0 + nrm((D,), 0.02),
    }


def reference(x, c, ctx, c_ctx, ada_w, ada_b, norm1_g, norm2_g, w_in, na_rpb, conv_w,
              s5_a_re, s5_a_im, s5_b_re, s5_b_im, s5_c_re, s5_c_im, s5_log_dt, s5_d, s5_w_glu,
              grp_g, w_out, peer_wq, peer_keys, peer_u, peer_v, final_g):
    for i in range(DEPTH):
        x, ctx = hybrid_layer(x, ctx, c, c_ctx, ada_w[i], ada_b[i], norm1_g[i], norm2_g[i], w_in[i],
                              na_rpb[i], conv_w[i], s5_a_re[i], s5_a_im[i], s5_b_re[i], s5_b_im[i],
                              s5_c_re[i], s5_c_im[i], s5_log_dt[i], s5_d[i], s5_w_glu[i], grp_g[i],
                              w_out[i], peer_wq[i], peer_keys[i], peer_u[i], peer_v[i],
                              update_ctx=(i < DEPTH - 1))
    return rmsnorm(x, final_g)
```

```python
import functools
import math

import jax
import jax.numpy as jnp
from jax import lax
from jax.experimental import pallas as pl
from jax.experimental.pallas import tpu as pltpu

f32 = jnp.float32
bf16 = jnp.bfloat16
HIGHEST = lax.Precision.HIGHEST

EPS = 1e-6
NEG = -1e30
GRID_W = 64
GROUP_W = 256
N_GROUPS = 4
NA_HEADS = 4
NA_HEAD_DIM = 64
NA_ROWS = 8
NA_COLS = 16
NA_QROWS = 8
NA_KROWS = 16
FOURIER_HEADS = 4
FOURIER_DIM = 64
S5_CH = 16
S5_GROUPS = 16
S5_STATE = 64
S5_WIDTH = S5_GROUPS * S5_STATE
S5_CHUNK = 128
PEER_HEADS = 8
PEER_KEYS = 128
PEER_TOPK = 16
PEER_HALF = 128
N_MOD = 6
LANES = 128


def _params(sem, vmem_mb=None):
    kw = dict(dimension_semantics=sem)
    if vmem_mb is not None:
        kw["vmem_limit_bytes"] = vmem_mb << 20
    return pltpu.CompilerParams(**kw)


def _norm_mod(x, g, sc, sh):
    y = x * lax.rsqrt(jnp.mean(x * x, axis=-1, keepdims=True) + EPS)
    return (y * g) * (1.0 + sc) + sh


def _ada_kernel(c_ref, w_ref, b_ref, o_ref):
    c = c_ref[...]
    s = c * jax.nn.sigmoid(c)
    o_ref[0] = jnp.dot(s, w_ref[0], preferred_element_type=f32, precision=HIGHEST) + b_ref[0]


def ada_modulation(cc, ada_w, ada_b):
    depth, d, n = ada_w.shape
    r = cc.shape[0]
    tn = 1024
    return pl.pallas_call(
        _ada_kernel,
        grid=(depth, n // tn),
        in_specs=[pl.BlockSpec((r, d), lambda i, j: (0, 0)),
                  pl.BlockSpec((1, d, tn), lambda i, j: (i, 0, j)),
                  pl.BlockSpec((1, 1, tn), lambda i, j: (i, 0, j))],
        out_specs=pl.BlockSpec((1, r, tn), lambda i, j: (i, 0, j)),
        out_shape=jax.ShapeDtypeStruct((depth, r, n), f32),
        compiler_params=_params(("arbitrary", "arbitrary")),
    )(cc, ada_w, ada_b.reshape(depth, 1, n))


def _norm_mm_kernel(x_ref, g_ref, sc_ref, sh_ref, w_ref, y_ref):
    h = _norm_mod(x_ref[0], g_ref[...], sc_ref[0], sh_ref[0])
    y_ref[0] = jnp.dot(h.astype(bf16), w_ref[...], preferred_element_type=f32).astype(y_ref.dtype)


def norm_mm(x, g, sc, sh, w):
    b, l, d = x.shape
    n = w.shape[1]
    tm = min(512, l)
    return pl.pallas_call(
        _norm_mm_kernel,
        grid=(b, l // tm),
        in_specs=[pl.BlockSpec((1, tm, d), lambda i, j: (i, j, 0)),
                  pl.BlockSpec((1, d), lambda i, j: (0, 0)),
                  pl.BlockSpec((1, 1, d), lambda i, j: (i, 0, 0)),
                  pl.BlockSpec((1, 1, d), lambda i, j: (i, 0, 0)),
                  pl.BlockSpec((d, n), lambda i, j: (0, 0))],
        out_specs=pl.BlockSpec((1, tm, n), lambda i, j: (i, j, 0)),
        out_shape=jax.ShapeDtypeStruct((b, l, n), bf16),
        compiler_params=_params(("arbitrary", "arbitrary"), 48),
    )(x, g.reshape(1, d), sc.reshape(b, 1, d), sh.reshape(b, 1, d), w)


def _na_bias_tables(rpb, rows):
    nj = rows // NA_QROWS
    qr = jnp.arange(NA_QROWS)
    kr = jnp.arange(NA_KROWS)
    col = jnp.arange(GRID_W)
    col_start = jnp.clip(col - NA_COLS // 2, 0, GRID_W - NA_COLS)
    col_ok = (col[None, :] >= col_start[:, None]) & (col[None, :] < col_start[:, None] + NA_COLS)
    d_col = jnp.clip(col[None, :] - col[:, None], -(NA_COLS - 1), NA_COLS - 1) + NA_COLS - 1
    oh_c = jax.nn.one_hot(d_col, 2 * NA_COLS - 1, dtype=f32)
    rp = rpb.astype(bf16).astype(f32)

    def table(j):
        r = NA_QROWS * j + qr
        ws = min(max(NA_QROWS * j - NA_ROWS // 2, 0), rows - NA_KROWS)
        krow = ws + kr
        rs = jnp.clip(r - NA_ROWS // 2, 0, rows - NA_ROWS)
        row_ok = (krow[None, :] >= rs[:, None]) & (krow[None, :] < rs[:, None] + NA_ROWS)
        d_row = jnp.clip(krow[None, :] - r[:, None] + NA_ROWS - 1, 0, 2 * NA_ROWS - 2)
        oh_r = jax.nn.one_hot(d_row, 2 * NA_ROWS - 1, dtype=f32)
        bias = jnp.einsum("hrc,qkr,xyc->hqxky", rp, oh_r, oh_c, precision=HIGHEST)
        ok = row_ok[:, None, :, None] & col_ok[None, :, None, :]
        bias = jnp.where(ok[None], bias, NEG)
        return bias.reshape(NA_HEADS, NA_QROWS * GRID_W, NA_KROWS * GRID_W).astype(bf16)

    return jnp.stack([table(0), table(min(1, nj - 1)), table(nj - 1)])


def _softmax_pv(qm, keys, vals, extra_bias, scale):
    ss = []
    for k, bias in zip(keys, extra_bias):
        s = lax.dot_general(qm, k, (((1,), (1,)), ((), ())), preferred_element_type=f32) * scale
        ss.append(s if bias is None else s + bias)
    m = functools.reduce(jnp.maximum, [jnp.max(s, axis=-1, keepdims=True) for s in ss])
    ps = [jnp.exp(s - m) for s in ss]
    l = functools.reduce(jnp.add, [jnp.sum(p, axis=-1, keepdims=True) for p in ps])
    o = functools.reduce(jnp.add, [jnp.dot(p.astype(bf16), v, preferred_element_type=f32)
                                   for p, v in zip(ps, vals)])
    return o * (1.0 / l)


def _na_kernel(q_ref, k_ref, v_ref, kc_ref, vc_ref, bias_ref, o_ref, *, rows):
    j = pl.program_id(0)
    ws = jnp.clip(NA_QROWS * j - NA_ROWS // 2, 0, rows - NA_KROWS)
    kstart = pl.multiple_of(ws * GRID_W, 256)
    nk = NA_KROWS * GRID_W
    q = q_ref[0]
    kw = k_ref[0, pl.ds(kstart, nk), :]
    vw = v_ref[0, pl.ds(kstart, nk), :]
    kc = kc_ref[0]
    vc = vc_ref[0]
    lane = lax.broadcasted_iota(jnp.int32, (1, GROUP_W), 1)
    scale = NA_HEAD_DIM ** -0.5
    out = jnp.zeros(q.shape, f32)
    for h in range(NA_HEADS):
        hm = (lane >= h * NA_HEAD_DIM) & (lane < (h + 1) * NA_HEAD_DIM)
        qm = jnp.where(hm, q, jnp.zeros_like(q))
        o = _softmax_pv(qm, [kw, kc], [vw, vc], [bias_ref[0, h].astype(f32), None], scale)
        out = out + jnp.where(hm, o, 0.0)
    o_ref[0] = out.astype(o_ref.dtype)


def neighbourhood_attention(y, yc, bias):
    b, l, _ = y.shape
    lc = yc.shape[1]
    rows = l // GRID_W
    nj = rows // NA_QROWS
    tq = NA_QROWS * GRID_W
    g = GROUP_W

    def bias_idx(j, i):
        return (jnp.where(j == 0, 0, jnp.where(j == nj - 1, 2, 1)), 0, 0, 0)

    return pl.pallas_call(
        functools.partial(_na_kernel, rows=rows),
        grid=(nj, b),
        in_specs=[pl.BlockSpec((1, tq, g), lambda j, i: (i, j, 0)),
                  pl.BlockSpec((1, l, g), lambda j, i: (i, 0, 1)),
                  pl.BlockSpec((1, l, g), lambda j, i: (i, 0, 2)),
                  pl.BlockSpec((1, lc, g), lambda j, i: (i, 0, 1)),
                  pl.BlockSpec((1, lc, g), lambda j, i: (i, 0, 2)),
                  pl.BlockSpec((1, NA_HEADS, tq, NA_KROWS * GRID_W), bias_idx)],
        out_specs=pl.BlockSpec((1, tq, g), lambda j, i: (i, j, 0)),
        out_shape=jax.ShapeDtypeStruct((b, l, g), bf16),
        compiler_params=_params(("arbitrary", "arbitrary"), 48),
    )(y, y, y, yc, yc, bias)


def _ctx_attn_kernel(q_ref, k_ref, v_ref, o_ref):
    q = q_ref[0]
    kc = k_ref[0]
    vc = v_ref[0]
    lane = lax.broadcasted_iota(jnp.int32, (1, GROUP_W), 1)
    scale = NA_HEAD_DIM ** -0.5
    out = jnp.zeros(q.shape, f32)
    for h in range(NA_HEADS):
        hm = (lane >= h * NA_HEAD_DIM) & (lane < (h + 1) * NA_HEAD_DIM)
        qm = jnp.where(hm, q, jnp.zeros_like(q))
        out = out + jnp.where(hm, _softmax_pv(qm, [kc], [vc], [None], scale), 0.0)
    o_ref[0] = out.astype(o_ref.dtype)


def ctx_attention(yc):
    b, lc, _ = yc.shape
    g = GROUP_W
    return pl.pallas_call(
        _ctx_attn_kernel,
        grid=(b,),
        in_specs=[pl.BlockSpec((1, lc, g), lambda i: (i, 0, 0)),
                  pl.BlockSpec((1, lc, g), lambda i: (i, 0, 1)),
                  pl.BlockSpec((1, lc, g), lambda i: (i, 0, 2))],
        out_specs=pl.BlockSpec((1, lc, g), lambda i: (i, 0, 0)),
        out_shape=jax.ShapeDtypeStruct((b, lc, g), bf16),
        compiler_params=_params(("arbitrary",)),
    )(yc, yc, yc)


def _conv_kernel(cb_ref, cc_ref, cx_ref, w_ref, o_ref):
    z = cc_ref[0].astype(f32) * cx_ref[0].astype(f32)
    l = z.shape[0]
    row = lax.broadcasted_iota(jnp.int32, z.shape, 0)
    zp = jnp.where(row == 0, 0.0, pltpu.roll(z, 1, 0))
    zn = jnp.where(row == l - 1, 0.0, pltpu.roll(z, l - 1, 0))
    w = w_ref[...]
    y = zp * w[0:1] + z * w[1:2] + zn * w[2:3]
    o_ref[0] = (cb_ref[0].astype(f32) * y).astype(o_ref.dtype)


def gated_conv(y, conv_w):
    b, l, _ = y.shape
    g = GROUP_W
    return pl.pallas_call(
        _conv_kernel,
        grid=(b,),
        in_specs=[pl.BlockSpec((1, l, g), lambda i: (i, 0, 3)),
                  pl.BlockSpec((1, l, g), lambda i: (i, 0, 4)),
                  pl.BlockSpec((1, l, g), lambda i: (i, 0, 5)),
                  pl.BlockSpec(conv_w.shape, lambda i: (0, 0))],
        out_specs=pl.BlockSpec((1, l, g), lambda i: (i, 0, 0)),
        out_shape=jax.ShapeDtypeStruct((b, l, g), bf16),
        compiler_params=_params(("arbitrary",), 48),
    )(y, y, y, conv_w)


def _dft_tables(l):
    il = jnp.arange(l, dtype=jnp.int32)
    ang = (2.0 * math.pi / l) * ((il[:, None] * il[None, :]) % l).astype(f32)
    cs = jnp.concatenate([jnp.cos(ang), jnp.sin(ang)], axis=1).astype(bf16)
    idd = jnp.arange(FOURIER_DIM, dtype=jnp.int32)
    angd = (2.0 * math.pi / FOURIER_DIM) * ((idd[:, None] * idd[None, :]) % FOURIER_DIM).astype(f32)
    eye = jnp.eye(FOURIER_HEADS, dtype=f32)
    bdc = jnp.kron(eye, jnp.cos(angd)).astype(bf16)
    bds = jnp.kron(eye, jnp.sin(angd)).astype(bf16)
    return cs, bdc, bds


def _chan_dft_kernel(z_ref, bdc_ref, bds_ref, o_ref):
    z = z_ref[0]
    l = z.shape[0]
    o_ref[0, 0:l, :] = jnp.dot(z, bdc_ref[...], preferred_element_type=f32).astype(o_ref.dtype)
    o_ref[0, l:2 * l, :] = (-jnp.dot(z, bds_ref[...], preferred_element_type=f32)).astype(o_ref.dtype)


def _pos_dft_kernel(cs_ref, z_ref, o_ref, *, scale):
    o_ref[0] = (jnp.dot(cs_ref[...], z_ref[0], preferred_element_type=f32) * scale).astype(o_ref.dtype)


def fourier_mix(y, tables):
    cs, bdc, bds = tables
    b, l, _ = y.shape
    g = GROUP_W
    zcs = pl.pallas_call(
        _chan_dft_kernel,
        grid=(b,),
        in_specs=[pl.BlockSpec((1, l, g), lambda i: (i, 0, 6)),
                  pl.BlockSpec((g, g), lambda i: (0, 0)),
                  pl.BlockSpec((g, g), lambda i: (0, 0))],
        out_specs=pl.BlockSpec((1, 2 * l, g), lambda i: (i, 0, 0)),
        out_shape=jax.ShapeDtypeStruct((b, 2 * l, g), bf16),
        compiler_params=_params(("arbitrary",), 48),
    )(y, bdc, bds)
    tl = min(512, l)
    return pl.pallas_call(
        functools.partial(_pos_dft_kernel, scale=(l * FOURIER_DIM) ** -0.5),
        grid=(l // tl, b),
        in_specs=[pl.BlockSpec((tl, 2 * l), lambda j, i: (j, 0)),
                  pl.BlockSpec((1, 2 * l, g), lambda j, i: (i, 0, 0))],
        out_specs=pl.BlockSpec((1, tl, g), lambda j, i: (i, j, 0)),
        out_shape=jax.ShapeDtypeStruct((b, l, g), bf16),
        compiler_params=_params(("arbitrary", "arbitrary"), 48),
    )(cs, zcs)


def _s5_matrices(a_re, a_im, b_re, b_im, c_re, c_im, log_dt):
    dt = jnp.exp(log_dt)[..., None]
    mag = jnp.exp(a_re * dt)
    abr = mag * jnp.cos(a_im * dt)
    abi = mag * jnp.sin(a_im * dt)
    den = a_re * a_re + a_im * a_im
    cr = ((abr - 1.0) * a_re + abi * a_im) / den
    ci = (abi * a_re - (abr - 1.0) * a_im) / den
    bbr = cr[..., None] * b_re - ci[..., None] * b_im
    bbi = cr[..., None] * b_im + ci[..., None] * b_re
    eye = jnp.eye(S5_GROUPS, dtype=f32)
    drive_re = jnp.einsum("dgph,gk->dghkp", bbr, eye).reshape(2, GROUP_W, S5_WIDTH)
    drive_im = jnp.einsum("dgph,gk->dghkp", bbi, eye).reshape(2, GROUP_W, S5_WIDTH)
    drive = jnp.concatenate([drive_re, drive_im], axis=-1).astype(bf16)
    read_re = jnp.einsum("dghp,gk->dgpkh", c_re, eye).reshape(2, S5_WIDTH, GROUP_W)
    read_im = jnp.einsum("dghp,gk->dgpkh", -c_im, eye).reshape(2, S5_WIDTH, GROUP_W)
    read = jnp.concatenate([read_re, read_im], axis=1).astype(bf16)
    a = jnp.stack([abr[0].reshape(-1), abi[0].reshape(-1), abr[1].reshape(-1), abi[1].reshape(-1)])
    return a, drive, read


def _s5_kernel(uf_ref, ub_ref, a_ref, drive_ref, read_ref, yf_ref, yb_ref, hf_ref, hb_ref, bf_ref, bb_ref,
               *, batch, chunk):
    s = S5_WIDTH

    @pl.when(pl.program_id(0) == 0)
    def _():
        hf_ref[...] = jnp.zeros_like(hf_ref)
        hb_ref[...] = jnp.zeros_like(hb_ref)

    bf_ref[...] = jnp.dot(uf_ref[...], drive_ref[0], preferred_element_type=f32)
    bb_ref[...] = jnp.dot(ub_ref[...], drive_ref[1], preferred_element_type=f32)

    def scan(buf, h_ref, ar, ai, reverse):
        ar = jnp.broadcast_to(ar, (batch, s))
        ai = jnp.broadcast_to(ai, (batch, s))

        def body(t, carry):
            hr, hi = carry
            step = (chunk - 1 - t) if reverse else t
            r0 = pl.multiple_of(step * batch, batch)
            nr = ar * hr - ai * hi + buf[pl.ds(r0, batch), 0:s]
            ni = ar * hi + ai * hr + buf[pl.ds(r0, batch), s:2 * s]
            buf[pl.ds(r0, batch), 0:s] = nr
            buf[pl.ds(r0, batch), s:2 * s] = ni
            return nr, ni

        hr, hi = lax.fori_loop(0, chunk, body, (h_ref[:, 0:s], h_ref[:, s:2 * s]))
        h_ref[:, 0:s] = hr
        h_ref[:, s:2 * s] = hi

    scan(bf_ref, hf_ref, a_ref[0:1, :], a_ref[1:2, :], False)
    scan(bb_ref, hb_ref, a_ref[2:3, :], a_ref[3:4, :], True)
    yf_ref[...] = jnp.dot(bf_ref[...].astype(bf16), read_ref[0], preferred_element_type=f32)
    yb_ref[...] = jnp.dot(bb_ref[...].astype(bf16), read_ref[1], preferred_element_type=f32)


def s5_scan(su_c, su_x, mats):
    a, drive, read = mats
    b, lc, g = su_c.shape
    l = su_x.shape[1]
    ch = S5_CHUNK
    nc, nx = lc // ch, l // ch
    lt = lc + l
    u = jnp.concatenate([su_c, su_x], axis=1).transpose(1, 0, 2).reshape(lt * b, g)
    rows = ch * b

    def bwd_idx(i):
        return (jnp.where(i < nc, nc - 1 - i, 2 * nc + nx - 1 - i), 0)

    yf, yb = pl.pallas_call(
        functools.partial(_s5_kernel, batch=b, chunk=ch),
        grid=(nc + nx,),
        in_specs=[pl.BlockSpec((rows, g), lambda i: (i, 0)),
                  pl.BlockSpec((rows, g), bwd_idx),
                  pl.BlockSpec(a.shape, lambda i: (0, 0)),
                  pl.BlockSpec(drive.shape, lambda i: (0, 0, 0)),
                  pl.BlockSpec(read.shape, lambda i: (0, 0, 0))],
        out_specs=[pl.BlockSpec((rows, g), lambda i: (i, 0)),
                   pl.BlockSpec((rows, g), bwd_idx)],
        out_shape=[jax.ShapeDtypeStruct((lt * b, g), f32)] * 2,
        scratch_shapes=[pltpu.VMEM((b, 2 * S5_WIDTH), f32), pltpu.VMEM((b, 2 * S5_WIDTH), f32),
                        pltpu.VMEM((rows, 2 * S5_WIDTH), f32), pltpu.VMEM((rows, 2 * S5_WIDTH), f32)],
        compiler_params=_params(("arbitrary",), 56),
    )(u, u, a, drive, read)
    yf = yf.reshape(lt, b, g).transpose(1, 0, 2)
    yb = yb.reshape(lt, b, g).transpose(1, 0, 2)
    return yf, yb


def _group_norm(p, g):
    return (p * lax.rsqrt(jnp.mean(p * p, axis=-1, keepdims=True) + EPS)) * g


def _merge_kernel(att_ref, conv_ref, four_ref, yf_ref, yb_ref, su_ref, x_ref, g1_ref, d_ref, wglu_ref,
                  gg_ref, wout_ref, o_ref):
    y = yf_ref[0] + yb_ref[0] + d_ref[...] * su_ref[0].astype(f32)
    gl = jax.nn.gelu(y)
    ssm = gl * jax.nn.sigmoid(jnp.dot(gl.astype(bf16), wglu_ref[...], preferred_element_type=f32))
    parts = [att_ref[0].astype(f32), conv_ref[0].astype(f32), four_ref[0].astype(f32), ssm]
    mix = None
    for k, p in enumerate(parts):
        pn = _group_norm(p, gg_ref[k:k + 1, :]).astype(bf16)
        t = jnp.dot(pn, wout_ref[k * GROUP_W:(k + 1) * GROUP_W, :], preferred_element_type=f32)
        mix = t if mix is None else mix + t
    o_ref[0] = x_ref[0] + g1_ref[0] * mix


def merge_groups(att, conv, four, yf, yb, y, x, g1, d_skip, w_glu, grp_g, w_out):
    b, l, d = x.shape
    g = GROUP_W
    tm = min(512, l)
    blk = lambda i, j: (i, j, 0)
    return pl.pallas_call(
        _merge_kernel,
        grid=(b, l // tm),
        in_specs=[pl.BlockSpec((1, tm, g), blk), pl.BlockSpec((1, tm, g), blk), pl.BlockSpec((1, tm, g), blk),
                  pl.BlockSpec((1, tm, g), blk), pl.BlockSpec((1, tm, g), blk),
                  pl.BlockSpec((1, tm, g), lambda i, j: (i, j, 7)),
                  pl.BlockSpec((1, tm, d), blk),
                  pl.BlockSpec((1, 1, d), lambda i, j: (i, 0, 0)),
                  pl.BlockSpec((1, g), lambda i, j: (0, 0)),
                  pl.BlockSpec((g, g), lambda i, j: (0, 0)),
                  pl.BlockSpec((N_GROUPS, g), lambda i, j: (0, 0)),
                  pl.BlockSpec((d, d), lambda i, j: (0, 0))],
        out_specs=pl.BlockSpec((1, tm, d), blk),
        out_shape=jax.ShapeDtypeStruct((b, l, d), f32),
        compiler_params=_params(("arbitrary", "arbitrary"), 48),
    )(att, conv, four, yf, yb, y, x, g1.reshape(b, 1, d), d_skip.reshape(1, g), w_glu, grp_g, w_out)


def _peer_front_kernel(x_ref, g_ref, sc_ref, sh_ref, wqt_ref, keys_ref, ht_ref, st_ref):
    h = _norm_mod(x_ref[0], g_ref[...], sc_ref[0], sh_ref[0])
    ht = h.T.astype(bf16)
    ht_ref[...] = ht
    qt = jnp.dot(wqt_ref[...], ht, preferred_element_type=f32).astype(bf16)
    for hz in range(2 * PEER_HEADS):
        rows = slice(hz * PEER_HALF, (hz + 1) * PEER_HALF)
        st_ref[rows, :] = jnp.dot(keys_ref[hz], qt[rows, :], preferred_element_type=f32)


def peer_front(x, g, sc, sh, wq_t, keys):
    b, l, d = x.shape
    nq = wq_t.shape[0]
    tm = min(512, l)
    nt = l // tm
    return pl.pallas_call(
        _peer_front_kernel,
        grid=(b, nt),
        in_specs=[pl.BlockSpec((1, tm, d), lambda i, j: (i, j, 0)),
                  pl.BlockSpec((1, d), lambda i, j: (0, 0)),
                  pl.BlockSpec((1, 1, d), lambda i, j: (i, 0, 0)),
                  pl.BlockSpec((1, 1, d), lambda i, j: (i, 0, 0)),
                  pl.BlockSpec((nq, d), lambda i, j: (0, 0)),
                  pl.BlockSpec(keys.shape, lambda i, j: (0, 0, 0))],
        out_specs=[pl.BlockSpec((d, tm), lambda i, j: (0, i * nt + j)),
                   pl.BlockSpec((nq, tm), lambda i, j: (0, i * nt + j))],
        out_shape=[jax.ShapeDtypeStruct((d, b * l), bf16), jax.ShapeDtypeStruct((nq, b * l), f32)],
        compiler_params=_params(("arbitrary", "arbitrary"), 48),
    )(x, g.reshape(1, d), sc.reshape(b, 1, d), sh.reshape(b, 1, d), wq_t, keys)


_CAND = [(j, k) for j in range(PEER_TOPK + 1) for k in range(PEER_TOPK + 1) if (j + 1) * (k + 1) <= PEER_TOPK + 1]
_NCAND = -(-len(_CAND) // 8) * 8


def _peer_select_kernel(st_ref, p1_ref, p2_ref, c_ref, a_scr, b_scr, cand_scr):
    nk = PEER_TOPK + 1

    def extract(s, out_scr):
        def body(k, s):
            m = jnp.max(s, axis=0, keepdims=True)
            out_scr[pl.ds(k, 1), :] = m
            return jnp.where(s >= m, NEG, s)
        lax.fori_loop(0, nk, body, s)

    def head(h, carry):
        r1 = pl.multiple_of(h * 2 * PEER_HALF, PEER_HALF)
        r2 = pl.multiple_of(h * 2 * PEER_HALF + PEER_HALF, PEER_HALF)
        s1 = st_ref[pl.ds(r1, PEER_HALF), :]
        s2 = st_ref[pl.ds(r2, PEER_HALF), :]
        extract(s1, a_scr)
        extract(s2, b_scr)
        cand_scr[...] = jnp.full(cand_scr.shape, NEG, f32)
        for idx, (j, k) in enumerate(_CAND):
            cand_scr[idx:idx + 1, :] = a_scr[j:j + 1, :] + b_scr[k:k + 1, :]
        cand = cand_scr[...]

        def body(k, carry):
            c, v16, v17 = carry
            m = jnp.max(c, axis=0, keepdims=True)
            v16 = jnp.where(k == PEER_TOPK - 1, m, v16)
            v17 = jnp.where(k == PEER_TOPK, m, v17)
            return jnp.where(c >= m, NEG, c), v16, v17

        zero = jnp.zeros((1, cand.shape[1]), f32)
        _, v16, v17 = lax.fori_loop(0, nk, body, (cand, zero, zero))
        a0 = a_scr[0:1, :]
        b0 = b_scr[0:1, :]
        top = a0 + b0
        z = jnp.sum(jnp.where(cand >= v16, jnp.exp(cand - top), 0.0), axis=0, keepdims=True)
        inv = 1.0 / z
        p1_ref[h] = jnp.exp(s1 - a0) * inv
        p2_ref[h] = jnp.exp(s2 - b0)
        c_ref[pl.ds(h, 1), :] = jnp.exp(0.5 * (v16 + v17) - top) * inv
        return carry

    lax.fori_loop(0, PEER_HEADS, head, 0)


def peer_select(st):
    nq, t = st.shape
    tl = LANES
    return pl.pallas_call(
        _peer_select_kernel,
        grid=(t // tl,),
        in_specs=[pl.BlockSpec((nq, tl), lambda i: (0, i))],
        out_specs=[pl.BlockSpec((PEER_HEADS, PEER_KEYS, tl), lambda i: (0, 0, i)),
                   pl.BlockSpec((PEER_HEADS, PEER_KEYS, tl), lambda i: (0, 0, i)),
                   pl.BlockSpec((PEER_HEADS, tl), lambda i: (0, i))],
        out_shape=[jax.ShapeDtypeStruct((PEER_HEADS, PEER_KEYS, t), f32),
                   jax.ShapeDtypeStruct((PEER_HEADS, PEER_KEYS, t), f32),
                   jax.ShapeDtypeStruct((PEER_HEADS, t), f32)],
        scratch_shapes=[pltpu.VMEM((24, tl), f32), pltpu.VMEM((24, tl), f32), pltpu.VMEM((_NCAND, tl), f32)],
        compiler_params=_params(("arbitrary",)),
    )(st)


def _peer_dense_kernel(ht_ref, u_ref, vt_ref, p1_ref, p2_ref, c_ref, x_ref, g2_ref, o_ref,
                       st_scr, a_scr, acc_scr, *, n_i1, tm):
    j = pl.program_id(1)

    @pl.when(j == 0)
    def _():
        acc_scr[...] = jnp.zeros_like(acc_scr)

    st_scr[...] = jnp.dot(u_ref[...], ht_ref[...], preferred_element_type=f32)

    for i1 in range(n_i1):
        rows = slice(i1 * PEER_KEYS, (i1 + 1) * PEER_KEYS)

        def lane_body(lg, carry):
            lanes = pl.ds(pl.multiple_of(lg * LANES, LANES), LANES)
            gate = jnp.zeros((PEER_KEYS, LANES), f32)
            for h in range(PEER_HEADS):
                w = p1_ref[h, i1:i1 + 1, lanes] * p2_ref[h, :, lanes]
                gate = gate + jnp.where(w >= c_ref[h:h + 1, lanes], w, 0.0)
            a_scr[rows, lanes] = (jax.nn.gelu(st_scr[rows, lanes]) * gate).astype(bf16)
            return carry

        lax.fori_loop(0, tm // LANES, lane_body, 0)

    acc_scr[...] += jnp.dot(vt_ref[...], a_scr[...], preferred_element_type=f32)

    @pl.when(j == pl.num_programs(1) - 1)
    def _():
        o_ref[...] = x_ref[...] + g2_ref[0] * acc_scr[...].T


def peer_dense(ht, u, vt, p1, p2, c, x2d, g2, tokens_per_batch):
    d, t = ht.shape
    e = u.shape[0]
    tm = min(512, tokens_per_batch)
    n_i1 = 8
    te = n_i1 * PEER_KEYS
    nb = tokens_per_batch // tm
    bsz = g2.shape[0]
    return pl.pallas_call(
        functools.partial(_peer_dense_kernel, n_i1=n_i1, tm=tm),
        grid=(t // tm, e // te),
        in_specs=[pl.BlockSpec((d, tm), lambda i, j: (0, i)),
                  pl.BlockSpec((te, d), lambda i, j: (j, 0)),
                  pl.BlockSpec((d, te), lambda i, j: (0, j)),
                  pl.BlockSpec((PEER_HEADS, n_i1, tm), lambda i, j: (0, j, i)),
                  pl.BlockSpec((PEER_HEADS, PEER_KEYS, tm), lambda i, j: (0, 0, i)),
                  pl.BlockSpec((PEER_HEADS, tm), lambda i, j: (0, i)),
                  pl.BlockSpec((tm, d), lambda i, j: (i, 0)),
                  pl.BlockSpec((1, 1, d), lambda i, j: (i // nb, 0, 0))],
        out_specs=pl.BlockSpec((tm, d), lambda i, j: (i, 0)),
        out_shape=jax.ShapeDtypeStruct((t, d), f32),
        scratch_shapes=[pltpu.VMEM((te, tm), f32), pltpu.VMEM((te, tm), bf16), pltpu.VMEM((d, tm), f32)],
        compiler_params=_params(("arbitrary", "arbitrary"), 56),
    )(ht, u, vt, p1, p2, c, x2d, g2.reshape(bsz, 1, d))


def peer_ffn_residual(x, g, sc, sh, g2, wq_t, keys, u, vt):
    b, l, d = x.shape
    ht, st = peer_front(x, g, sc, sh, wq_t, keys)
    p1, p2, c = peer_select(st)
    out = peer_dense(ht, u, vt, p1, p2, c, x.reshape(b * l, d), g2, l)
    return out.reshape(b, l, d)


def _final_norm_kernel(x_ref, g_ref, o_ref):
    x = x_ref[...]
    o_ref[...] = (x * lax.rsqrt(jnp.mean(x * x, axis=-1, keepdims=True) + EPS)) * g_ref[...]


def final_norm(x2d, g):
    t, d = x2d.shape
    tm = min(1024, t)
    return pl.pallas_call(
        _final_norm_kernel,
        grid=(t // tm,),
        in_specs=[pl.BlockSpec((tm, d), lambda i: (i, 0)),
                  pl.BlockSpec((1, d), lambda i: (0, 0))],
        out_specs=pl.BlockSpec((tm, d), lambda i: (i, 0)),
        out_shape=jax.ShapeDtypeStruct((t, d), x2d.dtype),
        compiler_params=_params(("arbitrary",)),
    )(x2d, g.reshape(1, d))


def kernel(x, c, ctx, c_ctx, ada_w, ada_b, norm1_g, norm2_g, w_in, na_rpb, conv_w, s5_a_re, s5_a_im, s5_b_re, s5_b_im, s5_c_re, s5_c_im, s5_log_dt, s5_d, s5_w_glu, grp_g, w_out, peer_wq, peer_keys, peer_u, peer_v, final_g):
    b, l, d = x.shape
    lc = ctx.shape[1]
    depth = ada_w.shape[0]
    g = GROUP_W

    n_rows = -(-(b + 1) // 8) * 8
    cc = jnp.zeros((n_rows, d), f32).at[:b].set(c).at[b].set(c_ctx)
    mods = ada_modulation(cc, ada_w, ada_b)

    dft_x = _dft_tables(l)
    dft_c = _dft_tables(lc)

    for i in range(depth):
        update_ctx = i < depth - 1
        sh1, sc1, g1, sh2, sc2, g2 = jnp.split(mods[i, :b], N_MOD, axis=-1)
        csh1, csc1, cg1, csh2, csc2, cg2 = [jnp.broadcast_to(m, (b, d))
                                            for m in jnp.split(mods[i, b:b + 1], N_MOD, axis=-1)]
        w_in_i = w_in[i].astype(bf16)
        w_out_i = w_out[i].astype(bf16)
        w_glu_i = s5_w_glu[i].astype(bf16)
        wq_t = peer_wq[i].T.astype(bf16)
        keys = peer_keys[i].reshape(2 * PEER_HEADS, PEER_KEYS, PEER_HALF).astype(bf16)
        u_i = peer_u[i].astype(bf16)
        vt_i = peer_v[i].T.astype(bf16)
        bias = _na_bias_tables(na_rpb[i], l // GRID_W)
        s5m = _s5_matrices(s5_a_re[i], s5_a_im[i], s5_b_re[i], s5_b_im[i], s5_c_re[i], s5_c_im[i], s5_log_dt[i])

        y = norm_mm(x, norm1_g[i], sc1, sh1, w_in_i)
        yc = norm_mm(ctx, norm1_g[i], csc1, csh1, w_in_i)

        att_x = neighbourhood_attention(y, yc, bias)
        conv_x = gated_conv(y, conv_w[i])
        four_x = fourier_mix(y, dft_x)
        yf, yb = s5_scan(yc[:, :, 7 * g:], y[:, :, 7 * g:], s5m)

        x = merge_groups(att_x, conv_x, four_x, yf[:, lc:], yb[:, lc:], y, x, g1,
                         s5_d[i], w_glu_i, grp_g[i], w_out_i)
        x = peer_ffn_residual(x, norm2_g[i], sc2, sh2, g2, wq_t, keys, u_i, vt_i)

        if update_ctx:
            att_c = ctx_attention(yc)
            conv_c = gated_conv(yc, conv_w[i])
            four_c = fourier_mix(yc, dft_c)
            ctx = merge_groups(att_c, conv_c, four_c, yf[:, :lc], yb[:, :lc], yc, ctx, cg1,
                               s5_d[i], w_glu_i, grp_g[i], w_out_i)
            ctx = peer_ffn_residual(ctx, norm2_g[i], csc2, csh2, cg2, wq_t, keys, u_i, vt_i)

    return final_norm(x.reshape(b * l, d), final_g).reshape(b, l, d)
```

```python
import functools
import math

import jax
import jax.numpy as jnp
from jax import lax
from jax.experimental import pallas as pl
from jax.experimental.pallas import tpu as pltpu

f32 = jnp.float32
bf16 = jnp.bfloat16
HIGHEST = lax.Precision.HIGHEST

EPS = 1e-6
NEG = -1e30
GRID_W = 64
GROUP_W = 256
N_GROUPS = 4
NA_HEADS = 4
NA_HEAD_DIM = 64
NA_ROWS = 8
NA_COLS = 16
NA_QROWS = 8
NA_KROWS = 16
FOURIER_HEADS = 4
FOURIER_DIM = 64
S5_CH = 16
S5_GROUPS = 16
S5_STATE = 64
S5_WIDTH = S5_GROUPS * S5_STATE
S5_CHUNK = 128
PEER_HEADS = 8
PEER_KEYS = 128
PEER_TOPK = 16
PEER_HALF = 128
PEER_SUB = 256
N_MOD = 6
LANES = 128


def _params(sem, vmem_mb=None):
    kw = dict(dimension_semantics=sem)
    if vmem_mb is not None:
        kw["vmem_limit_bytes"] = vmem_mb << 20
    return pltpu.CompilerParams(**kw)


def _norm_mod(x, g, sc, sh):
    y = x * lax.rsqrt(jnp.mean(x * x, axis=-1, keepdims=True) + EPS)
    return (y * g) * (1.0 + sc) + sh


def _ada_kernel(c_ref, w_ref, b_ref, o_ref):
    c = c_ref[...]
    s = c * jax.nn.sigmoid(c)
    o_ref[0] = jnp.dot(s, w_ref[0], preferred_element_type=f32, precision=HIGHEST) + b_ref[0]


def ada_modulation(cc, ada_w, ada_b):
    depth, d, n = ada_w.shape
    r = cc.shape[0]
    tn = 1024
    return pl.pallas_call(
        _ada_kernel,
        grid=(depth, n // tn),
        in_specs=[pl.BlockSpec((r, d), lambda i, j: (0, 0)),
                  pl.BlockSpec((1, d, tn), lambda i, j: (i, 0, j)),
                  pl.BlockSpec((1, 1, tn), lambda i, j: (i, 0, j))],
        out_specs=pl.BlockSpec((1, r, tn), lambda i, j: (i, 0, j)),
        out_shape=jax.ShapeDtypeStruct((depth, r, n), f32),
        compiler_params=_params(("arbitrary", "arbitrary")),
    )(cc, ada_w, ada_b.reshape(depth, 1, n))


def _norm_mm_kernel(x_ref, g_ref, sc_ref, sh_ref, w_ref, y_ref):
    h = _norm_mod(x_ref[0], g_ref[...], sc_ref[0], sh_ref[0])
    y_ref[0] = jnp.dot(h.astype(bf16), w_ref[...], preferred_element_type=f32).astype(y_ref.dtype)


def norm_mm(x, g, sc, sh, w):
    b, l, d = x.shape
    n = w.shape[1]
    tm = min(512, l)
    return pl.pallas_call(
        _norm_mm_kernel,
        name="norm_proj",
        grid=(b, l // tm),
        in_specs=[pl.BlockSpec((1, tm, d), lambda i, j: (i, j, 0)),
                  pl.BlockSpec((1, d), lambda i, j: (0, 0)),
                  pl.BlockSpec((1, 1, d), lambda i, j: (i, 0, 0)),
                  pl.BlockSpec((1, 1, d), lambda i, j: (i, 0, 0)),
                  pl.BlockSpec((d, n), lambda i, j: (0, 0))],
        out_specs=pl.BlockSpec((1, tm, n), lambda i, j: (i, j, 0)),
        out_shape=jax.ShapeDtypeStruct((b, l, n), bf16),
        compiler_params=_params(("arbitrary", "arbitrary"), 48),
    )(x, g.reshape(1, d), sc.reshape(b, 1, d), sh.reshape(b, 1, d), w)


def _na_bias_tables(rpb, rows):
    nj = rows // NA_QROWS
    qr = jnp.arange(NA_QROWS)
    kr = jnp.arange(NA_KROWS)
    col = jnp.arange(GRID_W)
    col_start = jnp.clip(col - NA_COLS // 2, 0, GRID_W - NA_COLS)
    col_ok = (col[None, :] >= col_start[:, None]) & (col[None, :] < col_start[:, None] + NA_COLS)
    d_col = jnp.clip(col[None, :] - col[:, None], -(NA_COLS - 1), NA_COLS - 1) + NA_COLS - 1
    oh_c = jax.nn.one_hot(d_col, 2 * NA_COLS - 1, dtype=f32)
    rp = rpb.astype(bf16).astype(f32)

    def table(j):
        r = NA_QROWS * j + qr
        ws = min(max(NA_QROWS * j - NA_ROWS // 2, 0), rows - NA_KROWS)
        krow = ws + kr
        rs = jnp.clip(r - NA_ROWS // 2, 0, rows - NA_ROWS)
        row_ok = (krow[None, :] >= rs[:, None]) & (krow[None, :] < rs[:, None] + NA_ROWS)
        d_row = jnp.clip(krow[None, :] - r[:, None] + NA_ROWS - 1, 0, 2 * NA_ROWS - 2)
        oh_r = jax.nn.one_hot(d_row, 2 * NA_ROWS - 1, dtype=f32)
        bias = jnp.einsum("hrc,qkr,xyc->hqxky", rp, oh_r, oh_c, precision=HIGHEST)
        ok = row_ok[:, None, :, None] & col_ok[None, :, None, :]
        bias = jnp.where(ok[None], bias, NEG)
        return bias.reshape(NA_HEADS, NA_QROWS * GRID_W, NA_KROWS * GRID_W).astype(bf16)

    return jnp.stack([table(0), table(min(1, nj - 1)), table(nj - 1)])


def _softmax_pv(qm, keys, vals, extra_bias, scale):
    ss = []
    for k, bias in zip(keys, extra_bias):
        s = lax.dot_general(qm, k, (((1,), (1,)), ((), ())), preferred_element_type=f32) * scale
        ss.append(s if bias is None else s + bias)
    m = functools.reduce(jnp.maximum, [jnp.max(s, axis=-1, keepdims=True) for s in ss])
    ps = [jnp.exp(s - m) for s in ss]
    l = functools.reduce(jnp.add, [jnp.sum(p, axis=-1, keepdims=True) for p in ps])
    o = functools.reduce(jnp.add, [jnp.dot(p.astype(bf16), v, preferred_element_type=f32)
                                   for p, v in zip(ps, vals)])
    return o * (1.0 / l)


def _na_kernel(q_ref, k_ref, v_ref, kc_ref, vc_ref, bias_ref, o_ref, *, rows):
    j = pl.program_id(0)
    ws = jnp.clip(NA_QROWS * j - NA_ROWS // 2, 0, rows - NA_KROWS)
    kstart = pl.multiple_of(ws * GRID_W, 256)
    nk = NA_KROWS * GRID_W
    q = q_ref[0]
    kw = k_ref[0, pl.ds(kstart, nk), :]
    vw = v_ref[0, pl.ds(kstart, nk), :]
    kc = kc_ref[0]
    vc = vc_ref[0]
    lane = lax.broadcasted_iota(jnp.int32, (1, GROUP_W), 1)
    scale = NA_HEAD_DIM ** -0.5
    out = jnp.zeros(q.shape, f32)
    for h in range(NA_HEADS):
        hm = (lane >= h * NA_HEAD_DIM) & (lane < (h + 1) * NA_HEAD_DIM)
        qm = jnp.where(hm, q, jnp.zeros_like(q))
        o = _softmax_pv(qm, [kw, kc], [vw, vc], [bias_ref[0, h].astype(f32), None], scale)
        out = out + jnp.where(hm, o, 0.0)
    o_ref[0] = out.astype(o_ref.dtype)


def neighbourhood_attention(y, yc, bias):
    b, l, _ = y.shape
    lc = yc.shape[1]
    rows = l // GRID_W
    nj = rows // NA_QROWS
    tq = NA_QROWS * GRID_W
    g = GROUP_W

    def bias_idx(j, i):
        return (jnp.where(j == 0, 0, jnp.where(j == nj - 1, 2, 1)), 0, 0, 0)

    return pl.pallas_call(
        functools.partial(_na_kernel, rows=rows),
        name="nbr_attn",
        grid=(nj, b),
        in_specs=[pl.BlockSpec((1, tq, g), lambda j, i: (i, j, 0)),
                  pl.BlockSpec((1, l, g), lambda j, i: (i, 0, 1)),
                  pl.BlockSpec((1, l, g), lambda j, i: (i, 0, 2)),
                  pl.BlockSpec((1, lc, g), lambda j, i: (i, 0, 1)),
                  pl.BlockSpec((1, lc, g), lambda j, i: (i, 0, 2)),
                  pl.BlockSpec((1, NA_HEADS, tq, NA_KROWS * GRID_W), bias_idx)],
        out_specs=pl.BlockSpec((1, tq, g), lambda j, i: (i, j, 0)),
        out_shape=jax.ShapeDtypeStruct((b, l, g), bf16),
        compiler_params=_params(("arbitrary", "arbitrary"), 48),
    )(y, y, y, yc, yc, bias)


def _ctx_attn_kernel(q_ref, k_ref, v_ref, o_ref):
    q = q_ref[0]
    kc = k_ref[0]
    vc = v_ref[0]
    lane = lax.broadcasted_iota(jnp.int32, (1, GROUP_W), 1)
    scale = NA_HEAD_DIM ** -0.5
    out = jnp.zeros(q.shape, f32)
    for h in range(NA_HEADS):
        hm = (lane >= h * NA_HEAD_DIM) & (lane < (h + 1) * NA_HEAD_DIM)
        qm = jnp.where(hm, q, jnp.zeros_like(q))
        out = out + jnp.where(hm, _softmax_pv(qm, [kc], [vc], [None], scale), 0.0)
    o_ref[0] = out.astype(o_ref.dtype)


def ctx_attention(yc):
    b, lc, _ = yc.shape
    g = GROUP_W
    return pl.pallas_call(
        _ctx_attn_kernel,
        grid=(b,),
        in_specs=[pl.BlockSpec((1, lc, g), lambda i: (i, 0, 0)),
                  pl.BlockSpec((1, lc, g), lambda i: (i, 0, 1)),
                  pl.BlockSpec((1, lc, g), lambda i: (i, 0, 2))],
        out_specs=pl.BlockSpec((1, lc, g), lambda i: (i, 0, 0)),
        out_shape=jax.ShapeDtypeStruct((b, lc, g), bf16),
        compiler_params=_params(("arbitrary",)),
    )(yc, yc, yc)


def _conv_kernel(cb_ref, cc_ref, cx_ref, w_ref, o_ref):
    z = cc_ref[0].astype(f32) * cx_ref[0].astype(f32)
    l = z.shape[0]
    row = lax.broadcasted_iota(jnp.int32, z.shape, 0)
    zp = jnp.where(row == 0, 0.0, pltpu.roll(z, 1, 0))
    zn = jnp.where(row == l - 1, 0.0, pltpu.roll(z, l - 1, 0))
    w = w_ref[...]
    y = zp * w[0:1] + z * w[1:2] + zn * w[2:3]
    o_ref[0] = (cb_ref[0].astype(f32) * y).astype(o_ref.dtype)


def gated_conv(y, conv_w):
    b, l, _ = y.shape
    g = GROUP_W
    return pl.pallas_call(
        _conv_kernel,
        grid=(b,),
        in_specs=[pl.BlockSpec((1, l, g), lambda i: (i, 0, 3)),
                  pl.BlockSpec((1, l, g), lambda i: (i, 0, 4)),
                  pl.BlockSpec((1, l, g), lambda i: (i, 0, 5)),
                  pl.BlockSpec(conv_w.shape, lambda i: (0, 0))],
        out_specs=pl.BlockSpec((1, l, g), lambda i: (i, 0, 0)),
        out_shape=jax.ShapeDtypeStruct((b, l, g), bf16),
        compiler_params=_params(("arbitrary",), 48),
    )(y, y, y, conv_w)


def _dft_tables(l):
    il = jnp.arange(l, dtype=jnp.int32)
    ang = (2.0 * math.pi / l) * ((il[:, None] * il[None, :]) % l).astype(f32)
    cs = jnp.concatenate([jnp.cos(ang), jnp.sin(ang)], axis=1).astype(bf16)
    idd = jnp.arange(FOURIER_DIM, dtype=jnp.int32)
    angd = (2.0 * math.pi / FOURIER_DIM) * ((idd[:, None] * idd[None, :]) % FOURIER_DIM).astype(f32)
    eye = jnp.eye(FOURIER_HEADS, dtype=f32)
    bdc = jnp.kron(eye, jnp.cos(angd)).astype(bf16)
    bds = jnp.kron(eye, jnp.sin(angd)).astype(bf16)
    return cs, bdc, bds


def _chan_dft_kernel(z_ref, bdc_ref, bds_ref, o_ref):
    z = z_ref[0]
    l = z.shape[0]
    o_ref[0, 0:l, :] = jnp.dot(z, bdc_ref[...], preferred_element_type=f32).astype(o_ref.dtype)
    o_ref[0, l:2 * l, :] = (-jnp.dot(z, bds_ref[...], preferred_element_type=f32)).astype(o_ref.dtype)


def _pos_dft_kernel(cs_ref, z_ref, o_ref, *, scale):
    o_ref[0] = (jnp.dot(cs_ref[...], z_ref[0], preferred_element_type=f32) * scale).astype(o_ref.dtype)


def fourier_mix(y, tables):
    cs, bdc, bds = tables
    b, l, _ = y.shape
    g = GROUP_W
    zcs = pl.pallas_call(
        _chan_dft_kernel,
        grid=(b,),
        in_specs=[pl.BlockSpec((1, l, g), lambda i: (i, 0, 6)),
                  pl.BlockSpec((g, g), lambda i: (0, 0)),
                  pl.BlockSpec((g, g), lambda i: (0, 0))],
        out_specs=pl.BlockSpec((1, 2 * l, g), lambda i: (i, 0, 0)),
        out_shape=jax.ShapeDtypeStruct((b, 2 * l, g), bf16),
        compiler_params=_params(("arbitrary",), 48),
    )(y, bdc, bds)
    tl = min(512, l)
    return pl.pallas_call(
        functools.partial(_pos_dft_kernel, scale=(l * FOURIER_DIM) ** -0.5),
        name="pos_dft",
        grid=(l // tl, b),
        in_specs=[pl.BlockSpec((tl, 2 * l), lambda j, i: (j, 0)),
                  pl.BlockSpec((1, 2 * l, g), lambda j, i: (i, 0, 0))],
        out_specs=pl.BlockSpec((1, tl, g), lambda j, i: (i, j, 0)),
        out_shape=jax.ShapeDtypeStruct((b, l, g), bf16),
        compiler_params=_params(("arbitrary", "arbitrary"), 48),
    )(cs, zcs)


def _s5_matrices(a_re, a_im, b_re, b_im, c_re, c_im, log_dt):
    dt = jnp.exp(log_dt)[..., None]
    mag = jnp.exp(a_re * dt)
    abr = mag * jnp.cos(a_im * dt)
    abi = mag * jnp.sin(a_im * dt)
    den = a_re * a_re + a_im * a_im
    cr = ((abr - 1.0) * a_re + abi * a_im) / den
    ci = (abi * a_re - (abr - 1.0) * a_im) / den
    bbr = cr[..., None] * b_re - ci[..., None] * b_im
    bbi = cr[..., None] * b_im + ci[..., None] * b_re
    eye = jnp.eye(S5_GROUPS, dtype=f32)
    drive_re = jnp.einsum("dgph,gk->dghkp", bbr, eye).reshape(2, GROUP_W, S5_WIDTH)
    drive_im = jnp.einsum("dgph,gk->dghkp", bbi, eye).reshape(2, GROUP_W, S5_WIDTH)
    drive = jnp.concatenate([drive_re, drive_im], axis=-1).astype(bf16)
    read_re = jnp.einsum("dghp,gk->dgpkh", c_re, eye).reshape(2, S5_WIDTH, GROUP_W)
    read_im = jnp.einsum("dghp,gk->dgpkh", -c_im, eye).reshape(2, S5_WIDTH, GROUP_W)
    read = jnp.concatenate([read_re, read_im], axis=1).astype(bf16)
    a = jnp.stack([abr[0].reshape(-1), abi[0].reshape(-1), abr[1].reshape(-1), abi[1].reshape(-1)])
    return a, drive, read


def _s5_kernel(uf_ref, ub_ref, a_ref, drive_ref, read_ref, yf_ref, yb_ref, hf_ref, hb_ref, bf_ref, bb_ref,
               *, batch, chunk):
    s = S5_WIDTH

    @pl.when(pl.program_id(0) == 0)
    def _():
        hf_ref[...] = jnp.zeros_like(hf_ref)
        hb_ref[...] = jnp.zeros_like(hb_ref)

    bf_ref[...] = jnp.dot(uf_ref[...], drive_ref[0], preferred_element_type=f32)
    bb_ref[...] = jnp.dot(ub_ref[...], drive_ref[1], preferred_element_type=f32)

    def scan(buf, h_ref, ar, ai, reverse):
        ar = jnp.broadcast_to(ar, (batch, s))
        ai = jnp.broadcast_to(ai, (batch, s))

        def body(t, carry):
            hr, hi = carry
            step = (chunk - 1 - t) if reverse else t
            r0 = pl.multiple_of(step * batch, batch)
            nr = ar * hr - ai * hi + buf[pl.ds(r0, batch), 0:s]
            ni = ar * hi + ai * hr + buf[pl.ds(r0, batch), s:2 * s]
            buf[pl.ds(r0, batch), 0:s] = nr
            buf[pl.ds(r0, batch), s:2 * s] = ni
            return nr, ni

        hr, hi = lax.fori_loop(0, chunk, body, (h_ref[:, 0:s], h_ref[:, s:2 * s]))
        h_ref[:, 0:s] = hr
        h_ref[:, s:2 * s] = hi

    scan(bf_ref, hf_ref, a_ref[0:1, :], a_ref[1:2, :], False)
    scan(bb_ref, hb_ref, a_ref[2:3, :], a_ref[3:4, :], True)
    yf_ref[...] = jnp.dot(bf_ref[...].astype(bf16), read_ref[0], preferred_element_type=f32)
    yb_ref[...] = jnp.dot(bb_ref[...].astype(bf16), read_ref[1], preferred_element_type=f32)


def s5_scan(su_c, su_x, mats):
    a, drive, read = mats
    b, lc, g = su_c.shape
    l = su_x.shape[1]
    ch = S5_CHUNK
    nc, nx = lc // ch, l // ch
    lt = lc + l
    u = jnp.concatenate([su_c, su_x], axis=1).transpose(1, 0, 2).reshape(lt * b, g)
    rows = ch * b

    def bwd_idx(i):
        return (jnp.where(i < nc, nc - 1 - i, 2 * nc + nx - 1 - i), 0)

    yf, yb = pl.pallas_call(
        functools.partial(_s5_kernel, batch=b, chunk=ch),
        name="s5_scan",
        grid=(nc + nx,),
        in_specs=[pl.BlockSpec((rows, g), lambda i: (i, 0)),
                  pl.BlockSpec((rows, g), bwd_idx),
                  pl.BlockSpec(a.shape, lambda i: (0, 0)),
                  pl.BlockSpec(drive.shape, lambda i: (0, 0, 0)),
                  pl.BlockSpec(read.shape, lambda i: (0, 0, 0))],
        out_specs=[pl.BlockSpec((rows, g), lambda i: (i, 0)),
                   pl.BlockSpec((rows, g), bwd_idx)],
        out_shape=[jax.ShapeDtypeStruct((lt * b, g), f32)] * 2,
        scratch_shapes=[pltpu.VMEM((b, 2 * S5_WIDTH), f32), pltpu.VMEM((b, 2 * S5_WIDTH), f32),
                        pltpu.VMEM((rows, 2 * S5_WIDTH), f32), pltpu.VMEM((rows, 2 * S5_WIDTH), f32)],
        compiler_params=_params(("arbitrary",), 56),
    )(u, u, a, drive, read)
    yf = yf.reshape(lt, b, g).transpose(1, 0, 2)
    yb = yb.reshape(lt, b, g).transpose(1, 0, 2)
    return yf, yb


def _group_norm(p, g):
    return (p * lax.rsqrt(jnp.mean(p * p, axis=-1, keepdims=True) + EPS)) * g


def _merge_kernel(att_ref, conv_ref, four_ref, yf_ref, yb_ref, su_ref, x_ref, g1_ref, d_ref, wglu_ref,
                  gg_ref, wout_ref, o_ref):
    y = yf_ref[0] + yb_ref[0] + d_ref[...] * su_ref[0].astype(f32)
    gl = jax.nn.gelu(y)
    ssm = gl * jax.nn.sigmoid(jnp.dot(gl.astype(bf16), wglu_ref[...], preferred_element_type=f32))
    parts = [att_ref[0].astype(f32), conv_ref[0].astype(f32), four_ref[0].astype(f32), ssm]
    mix = None
    for k, p in enumerate(parts):
        pn = _group_norm(p, gg_ref[k:k + 1, :]).astype(bf16)
        t = jnp.dot(pn, wout_ref[k * GROUP_W:(k + 1) * GROUP_W, :], preferred_element_type=f32)
        mix = t if mix is None else mix + t
    o_ref[0] = x_ref[0] + g1_ref[0] * mix


def merge_groups(att, conv, four, yf, yb, y, x, g1, d_skip, w_glu, grp_g, w_out):
    b, l, d = x.shape
    g = GROUP_W
    tm = min(512, l)
    blk = lambda i, j: (i, j, 0)
    return pl.pallas_call(
        _merge_kernel,
        name="merge_out",
        grid=(b, l // tm),
        in_specs=[pl.BlockSpec((1, tm, g), blk), pl.BlockSpec((1, tm, g), blk), pl.BlockSpec((1, tm, g), blk),
                  pl.BlockSpec((1, tm, g), blk), pl.BlockSpec((1, tm, g), blk),
                  pl.BlockSpec((1, tm, g), lambda i, j: (i, j, 7)),
                  pl.BlockSpec((1, tm, d), blk),
                  pl.BlockSpec((1, 1, d), lambda i, j: (i, 0, 0)),
                  pl.BlockSpec((1, g), lambda i, j: (0, 0)),
                  pl.BlockSpec((g, g), lambda i, j: (0, 0)),
                  pl.BlockSpec((N_GROUPS, g), lambda i, j: (0, 0)),
                  pl.BlockSpec((d, d), lambda i, j: (0, 0))],
        out_specs=pl.BlockSpec((1, tm, d), blk),
        out_shape=jax.ShapeDtypeStruct((b, l, d), f32),
        compiler_params=_params(("arbitrary", "arbitrary"), 48),
    )(att, conv, four, yf, yb, y, x, g1.reshape(b, 1, d), d_skip.reshape(1, g), w_glu, grp_g, w_out)


def _peer_front_kernel(x_ref, g_ref, sc_ref, sh_ref, wqt_ref, keys_ref, ht_ref, st_ref):
    h = _norm_mod(x_ref[0], g_ref[...], sc_ref[0], sh_ref[0])
    ht = h.T.astype(bf16)
    ht_ref[...] = ht
    qt = jnp.dot(wqt_ref[...], ht, preferred_element_type=f32).astype(bf16)
    for hz in range(2 * PEER_HEADS):
        rows = slice(hz * PEER_HALF, (hz + 1) * PEER_HALF)
        st_ref[rows, :] = jnp.dot(keys_ref[hz], qt[rows, :], preferred_element_type=f32)


def peer_front(x, g, sc, sh, wq_t, keys):
    b, l, d = x.shape
    nq = wq_t.shape[0]
    tm = min(512, l)
    nt = l // tm
    return pl.pallas_call(
        _peer_front_kernel,
        name="peer_front",
        grid=(b, nt),
        in_specs=[pl.BlockSpec((1, tm, d), lambda i, j: (i, j, 0)),
                  pl.BlockSpec((1, d), lambda i, j: (0, 0)),
                  pl.BlockSpec((1, 1, d), lambda i, j: (i, 0, 0)),
                  pl.BlockSpec((1, 1, d), lambda i, j: (i, 0, 0)),
                  pl.BlockSpec((nq, d), lambda i, j: (0, 0)),
                  pl.BlockSpec(keys.shape, lambda i, j: (0, 0, 0))],
        out_specs=[pl.BlockSpec((d, tm), lambda i, j: (0, i * nt + j)),
                   pl.BlockSpec((nq, tm), lambda i, j: (0, i * nt + j))],
        out_shape=[jax.ShapeDtypeStruct((d, b * l), bf16), jax.ShapeDtypeStruct((nq, b * l), f32)],
        compiler_params=_params(("arbitrary", "arbitrary"), 48),
    )(x, g.reshape(1, d), sc.reshape(b, 1, d), sh.reshape(b, 1, d), wq_t, keys)


_CAND = [(j, k) for j in range(PEER_TOPK + 1) for k in range(PEER_TOPK + 1) if (j + 1) * (k + 1) <= PEER_TOPK + 1]
BIG = 3e38


def _peer_select_kernel(st_ref, p1_ref, p2_ref, c_ref, ta_scr, tb_scr, cand_scr, *, tl):
    nk = PEER_TOPK + 1
    big = jnp.full((PEER_HEADS, LANES), BIG, f32)

    def lane_group(lg, carry):
        lanes = pl.ds(pl.multiple_of(lg * LANES, LANES), LANES)

        def extract(k, prev):
            nxt = []
            for z in range(2):
                ms = []
                for h in range(PEER_HEADS):
                    r0 = (2 * h + z) * PEER_HALF
                    sc = st_ref[r0:r0 + PEER_HALF, lanes]
                    ms.append(jnp.max(jnp.where(sc < prev[z][h:h + 1], sc, NEG), axis=0, keepdims=True))
                nxt.append(jnp.concatenate(ms, axis=0))
            ta_scr[k] = nxt[0]
            tb_scr[k] = nxt[1]
            return tuple(nxt)

        lax.fori_loop(0, nk, extract, (big, big))

        for idx, (j, k) in enumerate(_CAND):
            cand_scr[idx] = ta_scr[j] + tb_scr[k]

        def nth(k, carry):
            prev, v16, v17 = carry
            m = functools.reduce(jnp.maximum, [jnp.where(cand_scr[i] < prev, cand_scr[i], NEG)
                                               for i in range(len(_CAND))])
            return m, jnp.where(k == PEER_TOPK - 1, m, v16), jnp.where(k == PEER_TOPK, m, v17)

        _, v16, v17 = lax.fori_loop(0, nk, nth, (big, big, big))
        a0 = ta_scr[0]
        b0 = tb_scr[0]
        top = a0 + b0
        z = functools.reduce(jnp.add, [jnp.where(cand_scr[i] >= v16, jnp.exp(cand_scr[i] - top), 0.0)
                                       for i in range(len(_CAND))])
        inv = 1.0 / z
        c_ref[:, lanes] = jnp.exp(0.5 * (v16 + v17) - top) * inv
        for h in range(PEER_HEADS):
            r0 = 2 * h * PEER_HALF
            p1_ref[h, :, lanes] = jnp.exp(st_ref[r0:r0 + PEER_HALF, lanes] - a0[h:h + 1]) * inv[h:h + 1]
            p2_ref[h, :, lanes] = jnp.exp(st_ref[r0 + PEER_HALF:r0 + 2 * PEER_HALF, lanes] - b0[h:h + 1])
        return carry

    lax.fori_loop(0, tl // LANES, lane_group, 0)


def peer_select(st):
    nq, t = st.shape
    tl = min(512, t)
    return pl.pallas_call(
        functools.partial(_peer_select_kernel, tl=tl),
        name="peer_select",
        grid=(t // tl,),
        in_specs=[pl.BlockSpec((nq, tl), lambda i: (0, i))],
        out_specs=[pl.BlockSpec((PEER_HEADS, PEER_KEYS, tl), lambda i: (0, 0, i)),
                   pl.BlockSpec((PEER_HEADS, PEER_KEYS, tl), lambda i: (0, 0, i)),
                   pl.BlockSpec((PEER_HEADS, tl), lambda i: (0, i))],
        out_shape=[jax.ShapeDtypeStruct((PEER_HEADS, PEER_KEYS, t), f32),
                   jax.ShapeDtypeStruct((PEER_HEADS, PEER_KEYS, t), f32),
                   jax.ShapeDtypeStruct((PEER_HEADS, t), f32)],
        scratch_shapes=[pltpu.VMEM((PEER_TOPK + 1, PEER_HEADS, LANES), f32),
                        pltpu.VMEM((PEER_TOPK + 1, PEER_HEADS, LANES), f32),
                        pltpu.VMEM((len(_CAND), PEER_HEADS, LANES), f32)],
        compiler_params=_params(("arbitrary",), 48),
    )(st)


def _peer_dense_kernel(ht_ref, u_ref, vt_ref, p1_ref, p2_ref, c_ref, x_ref, g2_ref, o_ref,
                       st_scr, a_scr, acc_scr, *, n_i1, tm, ne):
    s = pl.program_id(0)
    j3 = lax.rem(jnp.maximum(s - 2, 0), ne)

    @pl.when(s == 0)
    def _():
        st_scr[...] = jnp.zeros_like(st_scr)
        a_scr[...] = jnp.zeros_like(a_scr)

    first = j3 == 0
    for nt in range(tm // PEER_SUB):
        cols = slice(nt * PEER_SUB, (nt + 1) * PEER_SUB)
        upd = jnp.dot(vt_ref[...], a_scr[:, cols], preferred_element_type=f32)
        acc_scr[:, cols] = jnp.where(first, upd, acc_scr[:, cols] + upd)
        for i1 in range(n_i1):
            rows = slice(i1 * PEER_KEYS, (i1 + 1) * PEER_KEYS)
            for lg in range(PEER_SUB // LANES):
                lanes = slice(nt * PEER_SUB + lg * LANES, nt * PEER_SUB + (lg + 1) * LANES)
                gate = None
                for h in range(PEER_HEADS):
                    w = p1_ref[h, i1:i1 + 1, lanes] * p2_ref[h, :, lanes]
                    sel = jnp.where(w >= c_ref[h:h + 1, lanes], w, 0.0)
                    gate = sel if gate is None else gate + sel
                a_scr[rows, lanes] = (jax.nn.gelu(st_scr[rows, lanes]) * gate).astype(bf16)
        st_scr[:, cols] = jnp.dot(u_ref[...], ht_ref[:, cols], preferred_element_type=f32)

    @pl.when((s >= 2) & (j3 == ne - 1))
    def _():
        o_ref[...] = x_ref[...] + g2_ref[0] * acc_scr[...].T


def peer_dense(ht, u, vt, p1, p2, c, x2d, g2, tokens_per_batch):
    d, t = ht.shape
    e = u.shape[0]
    tm = min(512, tokens_per_batch)
    n_i1 = 8
    te = n_i1 * PEER_KEYS
    ne = e // te
    nb = tokens_per_batch // tm
    n = (t // tm) * ne
    bsz = g2.shape[0]

    def stage(lag):
        def split(s):
            q = jnp.clip(s - lag, 0, n - 1)
            return q // ne, lax.rem(q, ne)
        return split

    s1, s2, s3 = stage(0), stage(1), stage(2)
    return pl.pallas_call(
        functools.partial(_peer_dense_kernel, n_i1=n_i1, tm=tm, ne=ne),
        name="peer_dense",
        grid=(n + 2,),
        in_specs=[pl.BlockSpec((d, tm), lambda s: (0, s1(s)[0])),
                  pl.BlockSpec((te, d), lambda s: (s1(s)[1], 0)),
                  pl.BlockSpec((d, te), lambda s: (0, s3(s)[1])),
                  pl.BlockSpec((PEER_HEADS, n_i1, tm), lambda s: (0, s2(s)[1], s2(s)[0])),
                  pl.BlockSpec((PEER_HEADS, PEER_KEYS, tm), lambda s: (0, 0, s2(s)[0])),
                  pl.BlockSpec((PEER_HEADS, tm), lambda s: (0, s2(s)[0])),
                  pl.BlockSpec((tm, d), lambda s: (s3(s)[0], 0)),
                  pl.BlockSpec((1, 1, d), lambda s: (s3(s)[0] // nb, 0, 0))],
        out_specs=pl.BlockSpec((tm, d), lambda s: (s3(s)[0], 0)),
        out_shape=jax.ShapeDtypeStruct((t, d), f32),
        scratch_shapes=[pltpu.VMEM((te, tm), f32), pltpu.VMEM((te, tm), bf16), pltpu.VMEM((d, tm), f32)],
        compiler_params=_params(("arbitrary",), 56),
    )(ht, u, vt, p1, p2, c, x2d, g2.reshape(bsz, 1, d))


def peer_ffn_residual(x, g, sc, sh, g2, wq_t, keys, u, vt):
    b, l, d = x.shape
    ht, st = peer_front(x, g, sc, sh, wq_t, keys)
    p1, p2, c = peer_select(st)
    out = peer_dense(ht, u, vt, p1, p2, c, x.reshape(b * l, d), g2, l)
    return out.reshape(b, l, d)


def _final_norm_kernel(x_ref, g_ref, o_ref):
    x = x_ref[...]
    o_ref[...] = (x * lax.rsqrt(jnp.mean(x * x, axis=-1, keepdims=True) + EPS)) * g_ref[...]


def final_norm(x2d, g):
    t, d = x2d.shape
    tm = min(1024, t)
    return pl.pallas_call(
        _final_norm_kernel,
        grid=(t // tm,),
        in_specs=[pl.BlockSpec((tm, d), lambda i: (i, 0)),
                  pl.BlockSpec((1, d), lambda i: (0, 0))],
        out_specs=pl.BlockSpec((tm, d), lambda i: (i, 0)),
        out_shape=jax.ShapeDtypeStruct((t, d), x2d.dtype),
        compiler_params=_params(("arbitrary",)),
    )(x2d, g.reshape(1, d))


def kernel(x, c, ctx, c_ctx, ada_w, ada_b, norm1_g, norm2_g, w_in, na_rpb, conv_w, s5_a_re, s5_a_im, s5_b_re, s5_b_im, s5_c_re, s5_c_im, s5_log_dt, s5_d, s5_w_glu, grp_g, w_out, peer_wq, peer_keys, peer_u, peer_v, final_g):
    b, l, d = x.shape
    lc = ctx.shape[1]
    depth = ada_w.shape[0]
    g = GROUP_W

    n_rows = -(-(b + 1) // 8) * 8
    cc = jnp.zeros((n_rows, d), f32).at[:b].set(c).at[b].set(c_ctx)
    mods = ada_modulation(cc, ada_w, ada_b)

    dft_x = _dft_tables(l)
    dft_c = _dft_tables(lc)

    for i in range(depth):
        update_ctx = i < depth - 1
        sh1, sc1, g1, sh2, sc2, g2 = jnp.split(mods[i, :b], N_MOD, axis=-1)
        csh1, csc1, cg1, csh2, csc2, cg2 = [jnp.broadcast_to(m, (b, d))
                                            for m in jnp.split(mods[i, b:b + 1], N_MOD, axis=-1)]
        w_in_i = w_in[i].astype(bf16)
        w_out_i = w_out[i].astype(bf16)
        w_glu_i = s5_w_glu[i].astype(bf16)
        wq_t = peer_wq[i].T.astype(bf16)
        keys = peer_keys[i].reshape(2 * PEER_HEADS, PEER_KEYS, PEER_HALF).astype(bf16)
        u_i = peer_u[i].astype(bf16)
        vt_i = peer_v[i].T.astype(bf16)
        bias = _na_bias_tables(na_rpb[i], l // GRID_W)
        s5m = _s5_matrices(s5_a_re[i], s5_a_im[i], s5_b_re[i], s5_b_im[i], s5_c_re[i], s5_c_im[i], s5_log_dt[i])

        y = norm_mm(x, norm1_g[i], sc1, sh1, w_in_i)
        yc = norm_mm(ctx, norm1_g[i], csc1, csh1, w_in_i)

        att_x = neighbourhood_attention(y, yc, bias)
        conv_x = gated_conv(y, conv_w[i])
        four_x = fourier_mix(y, dft_x)
        yf, yb = s5_scan(yc[:, :, 7 * g:], y[:, :, 7 * g:], s5m)

        x = merge_groups(att_x, conv_x, four_x, yf[:, lc:], yb[:, lc:], y, x, g1,
                         s5_d[i], w_glu_i, grp_g[i], w_out_i)
        x = peer_ffn_residual(x, norm2_g[i], sc2, sh2, g2, wq_t, keys, u_i, vt_i)

        if update_ctx:
            att_c = ctx_attention(yc)
            conv_c = gated_conv(yc, conv_w[i])
            four_c = fourier_mix(yc, dft_c)
            ctx = merge_groups(att_c, conv_c, four_c, yf[:, :lc], yb[:, :lc], yc, ctx, cg1,
                               s5_d[i], w_glu_i, grp_g[i], w_out_i)
            ctx = peer_ffn_residual(ctx, norm2_g[i], csc2, csh2, cg2, wq_t, keys, u_i, vt_i)

    return final_norm(x.reshape(b * l, d), final_g).reshape(b, l, d)
```

```python
import functools
import math

import jax
import jax.numpy as jnp
from jax import lax
from jax.experimental import pallas as pl
from jax.experimental.pallas import tpu as pltpu

f32 = jnp.float32
bf16 = jnp.bfloat16
HIGHEST = lax.Precision.HIGHEST

EPS = 1e-6
NEG = -1e30
GRID_W = 64
GROUP_W = 256
N_GROUPS = 4
NA_HEADS = 4
NA_HEAD_DIM = 64
NA_ROWS = 8
NA_COLS = 16
NA_QROWS = 8
NA_KROWS = 16
FOURIER_HEADS = 4
FOURIER_DIM = 64
S5_CH = 16
S5_GROUPS = 16
S5_STATE = 64
S5_WIDTH = S5_GROUPS * S5_STATE
S5_CHUNK = 128
PEER_HEADS = 8
PEER_KEYS = 128
PEER_TOPK = 16
PEER_HALF = 128
PEER_CHUNK = 1024
PEER_SUB = 256
PEER_PIECES = 4
PEER_GATE_TILES = 2
N_MOD = 6
LANES = 128
SUBLANES = 8
BF16_ROWS = 16


def _params(sem, vmem_mb=None, flags=None):
    kw = dict(dimension_semantics=sem)
    if vmem_mb is not None:
        kw["vmem_limit_bytes"] = vmem_mb << 20
    if flags is not None:
        kw["flags"] = flags
    return pltpu.CompilerParams(**kw)


def _norm_mod(x, g, sc, sh):
    y = x * lax.rsqrt(jnp.mean(x * x, axis=-1, keepdims=True) + EPS)
    return (y * g) * (1.0 + sc) + sh


def _ada_kernel(c_ref, w_ref, b_ref, o_ref):
    c = c_ref[...]
    s = c * jax.nn.sigmoid(c)
    o_ref[0] = jnp.dot(s, w_ref[0], preferred_element_type=f32, precision=HIGHEST) + b_ref[0]


def ada_modulation(cc, ada_w, ada_b):
    depth, d, n = ada_w.shape
    r = cc.shape[0]
    tn = 1024
    return pl.pallas_call(
        _ada_kernel,
        grid=(depth, n // tn),
        in_specs=[pl.BlockSpec((r, d), lambda i, j: (0, 0)),
                  pl.BlockSpec((1, d, tn), lambda i, j: (i, 0, j)),
                  pl.BlockSpec((1, 1, tn), lambda i, j: (i, 0, j))],
        out_specs=pl.BlockSpec((1, r, tn), lambda i, j: (i, 0, j)),
        out_shape=jax.ShapeDtypeStruct((depth, r, n), f32),
        compiler_params=_params(("arbitrary", "arbitrary")),
    )(cc, ada_w, ada_b.reshape(depth, 1, n))


def _norm_mm_kernel(x_ref, g_ref, sc_ref, sh_ref, w_ref, y_ref):
    h = _norm_mod(x_ref[0], g_ref[...], sc_ref[0], sh_ref[0])
    y_ref[0] = jnp.dot(h.astype(bf16), w_ref[...], preferred_element_type=f32).astype(y_ref.dtype)


def norm_mm(x, g, sc, sh, w):
    b, l, d = x.shape
    n = w.shape[1]
    tm = min(512, l)
    return pl.pallas_call(
        _norm_mm_kernel,
        name="norm_proj",
        grid=(b, l // tm),
        in_specs=[pl.BlockSpec((1, tm, d), lambda i, j: (i, j, 0)),
                  pl.BlockSpec((1, d), lambda i, j: (0, 0)),
                  pl.BlockSpec((1, 1, d), lambda i, j: (i, 0, 0)),
                  pl.BlockSpec((1, 1, d), lambda i, j: (i, 0, 0)),
                  pl.BlockSpec((d, n), lambda i, j: (0, 0))],
        out_specs=pl.BlockSpec((1, tm, n), lambda i, j: (i, j, 0)),
        out_shape=jax.ShapeDtypeStruct((b, l, n), bf16),
        compiler_params=_params(("arbitrary", "arbitrary"), 48),
    )(x, g.reshape(1, d), sc.reshape(b, 1, d), sh.reshape(b, 1, d), w)


def _na_bias_tables(rpb, rows):
    nj = rows // NA_QROWS
    qr = jnp.arange(NA_QROWS)
    kr = jnp.arange(NA_KROWS)
    col = jnp.arange(GRID_W)
    col_start = jnp.clip(col - NA_COLS // 2, 0, GRID_W - NA_COLS)
    col_ok = (col[None, :] >= col_start[:, None]) & (col[None, :] < col_start[:, None] + NA_COLS)
    d_col = jnp.clip(col[None, :] - col[:, None], -(NA_COLS - 1), NA_COLS - 1) + NA_COLS - 1
    oh_c = jax.nn.one_hot(d_col, 2 * NA_COLS - 1, dtype=f32)
    rp = rpb.astype(bf16).astype(f32)

    def table(j):
        r = NA_QROWS * j + qr
        ws = min(max(NA_QROWS * j - NA_ROWS // 2, 0), rows - NA_KROWS)
        krow = ws + kr
        rs = jnp.clip(r - NA_ROWS // 2, 0, rows - NA_ROWS)
        row_ok = (krow[None, :] >= rs[:, None]) & (krow[None, :] < rs[:, None] + NA_ROWS)
        d_row = jnp.clip(krow[None, :] - r[:, None] + NA_ROWS - 1, 0, 2 * NA_ROWS - 2)
        oh_r = jax.nn.one_hot(d_row, 2 * NA_ROWS - 1, dtype=f32)
        bias = jnp.einsum("hrc,qkr,xyc->hqxky", rp, oh_r, oh_c, precision=HIGHEST)
        ok = row_ok[:, None, :, None] & col_ok[None, :, None, :]
        bias = jnp.where(ok[None], bias, NEG)
        return bias.reshape(NA_HEADS, NA_QROWS * GRID_W, NA_KROWS * GRID_W).astype(bf16)

    return jnp.stack([table(0), table(min(1, nj - 1)), table(nj - 1)])


def _softmax_pv(qm, keys, vals, extra_bias, scale):
    ss = []
    for k, bias in zip(keys, extra_bias):
        s = lax.dot_general(qm, k, (((1,), (1,)), ((), ())), preferred_element_type=f32) * scale
        ss.append(s if bias is None else s + bias)
    m = functools.reduce(jnp.maximum, [jnp.max(s, axis=-1, keepdims=True) for s in ss])
    ps = [jnp.exp(s - m) for s in ss]
    l = functools.reduce(jnp.add, [jnp.sum(p, axis=-1, keepdims=True) for p in ps])
    o = functools.reduce(jnp.add, [jnp.dot(p.astype(bf16), v, preferred_element_type=f32)
                                   for p, v in zip(ps, vals)])
    return o * (1.0 / l)


def _na_kernel(q_ref, k_ref, v_ref, kc_ref, vc_ref, bias_ref, o_ref, *, rows):
    j = pl.program_id(0)
    ws = jnp.clip(NA_QROWS * j - NA_ROWS // 2, 0, rows - NA_KROWS)
    kstart = pl.multiple_of(ws * GRID_W, 256)
    nk = NA_KROWS * GRID_W
    q = q_ref[0]
    kw = k_ref[0, pl.ds(kstart, nk), :]
    vw = v_ref[0, pl.ds(kstart, nk), :]
    kc = kc_ref[0]
    vc = vc_ref[0]
    lane = lax.broadcasted_iota(jnp.int32, (1, GROUP_W), 1)
    scale = NA_HEAD_DIM ** -0.5
    out = jnp.zeros(q.shape, f32)
    for h in range(NA_HEADS):
        hm = (lane >= h * NA_HEAD_DIM) & (lane < (h + 1) * NA_HEAD_DIM)
        qm = jnp.where(hm, q, jnp.zeros_like(q))
        o = _softmax_pv(qm, [kw, kc], [vw, vc], [bias_ref[0, h].astype(f32), None], scale)
        out = out + jnp.where(hm, o, 0.0)
    o_ref[0] = out.astype(o_ref.dtype)


def neighbourhood_attention(y, yc, bias):
    b, l, _ = y.shape
    lc = yc.shape[1]
    rows = l // GRID_W
    nj = rows // NA_QROWS
    tq = NA_QROWS * GRID_W
    g = GROUP_W

    def bias_idx(j, i):
        return (jnp.where(j == 0, 0, jnp.where(j == nj - 1, 2, 1)), 0, 0, 0)

    return pl.pallas_call(
        functools.partial(_na_kernel, rows=rows),
        name="nbr_attn",
        grid=(nj, b),
        in_specs=[pl.BlockSpec((1, tq, g), lambda j, i: (i, j, 0)),
                  pl.BlockSpec((1, l, g), lambda j, i: (i, 0, 1)),
                  pl.BlockSpec((1, l, g), lambda j, i: (i, 0, 2)),
                  pl.BlockSpec((1, lc, g), lambda j, i: (i, 0, 1)),
                  pl.BlockSpec((1, lc, g), lambda j, i: (i, 0, 2)),
                  pl.BlockSpec((1, NA_HEADS, tq, NA_KROWS * GRID_W), bias_idx)],
        out_specs=pl.BlockSpec((1, tq, g), lambda j, i: (i, j, 0)),
        out_shape=jax.ShapeDtypeStruct((b, l, g), bf16),
        compiler_params=_params(("arbitrary", "arbitrary"), 48),
    )(y, y, y, yc, yc, bias)


def _ctx_attn_kernel(q_ref, k_ref, v_ref, o_ref):
    q = q_ref[0]
    kc = k_ref[0]
    vc = v_ref[0]
    lane = lax.broadcasted_iota(jnp.int32, (1, GROUP_W), 1)
    scale = NA_HEAD_DIM ** -0.5
    out = jnp.zeros(q.shape, f32)
    for h in range(NA_HEADS):
        hm = (lane >= h * NA_HEAD_DIM) & (lane < (h + 1) * NA_HEAD_DIM)
        qm = jnp.where(hm, q, jnp.zeros_like(q))
        out = out + jnp.where(hm, _softmax_pv(qm, [kc], [vc], [None], scale), 0.0)
    o_ref[0] = out.astype(o_ref.dtype)


def ctx_attention(yc):
    b, lc, _ = yc.shape
    g = GROUP_W
    return pl.pallas_call(
        _ctx_attn_kernel,
        grid=(b,),
        in_specs=[pl.BlockSpec((1, lc, g), lambda i: (i, 0, 0)),
                  pl.BlockSpec((1, lc, g), lambda i: (i, 0, 1)),
                  pl.BlockSpec((1, lc, g), lambda i: (i, 0, 2))],
        out_specs=pl.BlockSpec((1, lc, g), lambda i: (i, 0, 0)),
        out_shape=jax.ShapeDtypeStruct((b, lc, g), bf16),
        compiler_params=_params(("arbitrary",)),
    )(yc, yc, yc)


def _conv_kernel(cb_ref, cc_ref, cx_ref, w_ref, o_ref):
    z = cc_ref[0].astype(f32) * cx_ref[0].astype(f32)
    l = z.shape[0]
    row = lax.broadcasted_iota(jnp.int32, z.shape, 0)
    zp = jnp.where(row == 0, 0.0, pltpu.roll(z, 1, 0))
    zn = jnp.where(row == l - 1, 0.0, pltpu.roll(z, l - 1, 0))
    w = w_ref[...]
    y = zp * w[0:1] + z * w[1:2] + zn * w[2:3]
    o_ref[0] = (cb_ref[0].astype(f32) * y).astype(o_ref.dtype)


def gated_conv(y, conv_w):
    b, l, _ = y.shape
    g = GROUP_W
    return pl.pallas_call(
        _conv_kernel,
        grid=(b,),
        in_specs=[pl.BlockSpec((1, l, g), lambda i: (i, 0, 3)),
                  pl.BlockSpec((1, l, g), lambda i: (i, 0, 4)),
                  pl.BlockSpec((1, l, g), lambda i: (i, 0, 5)),
                  pl.BlockSpec(conv_w.shape, lambda i: (0, 0))],
        out_specs=pl.BlockSpec((1, l, g), lambda i: (i, 0, 0)),
        out_shape=jax.ShapeDtypeStruct((b, l, g), bf16),
        compiler_params=_params(("arbitrary",), 48),
    )(y, y, y, conv_w)


def _dft_tables(l):
    il = jnp.arange(l, dtype=jnp.int32)
    ang = (2.0 * math.pi / l) * ((il[:, None] * il[None, :]) % l).astype(f32)
    cs = jnp.concatenate([jnp.cos(ang), jnp.sin(ang)], axis=1).astype(bf16)
    idd = jnp.arange(FOURIER_DIM, dtype=jnp.int32)
    angd = (2.0 * math.pi / FOURIER_DIM) * ((idd[:, None] * idd[None, :]) % FOURIER_DIM).astype(f32)
    eye = jnp.eye(FOURIER_HEADS, dtype=f32)
    bdc = jnp.kron(eye, jnp.cos(angd)).astype(bf16)
    bds = jnp.kron(eye, jnp.sin(angd)).astype(bf16)
    return cs, bdc, bds


def _chan_dft_kernel(z_ref, bdc_ref, bds_ref, o_ref):
    z = z_ref[0]
    l = z.shape[0]
    o_ref[0, 0:l, :] = jnp.dot(z, bdc_ref[...], preferred_element_type=f32).astype(o_ref.dtype)
    o_ref[0, l:2 * l, :] = (-jnp.dot(z, bds_ref[...], preferred_element_type=f32)).astype(o_ref.dtype)


def _pos_dft_kernel(cs_ref, z_ref, o_ref, *, scale):
    o_ref[0] = (jnp.dot(cs_ref[...], z_ref[0], preferred_element_type=f32) * scale).astype(o_ref.dtype)


def fourier_mix(y, tables):
    cs, bdc, bds = tables
    b, l, _ = y.shape
    g = GROUP_W
    zcs = pl.pallas_call(
        _chan_dft_kernel,
        grid=(b,),
        in_specs=[pl.BlockSpec((1, l, g), lambda i: (i, 0, 6)),
                  pl.BlockSpec((g, g), lambda i: (0, 0)),
                  pl.BlockSpec((g, g), lambda i: (0, 0))],
        out_specs=pl.BlockSpec((1, 2 * l, g), lambda i: (i, 0, 0)),
        out_shape=jax.ShapeDtypeStruct((b, 2 * l, g), bf16),
        compiler_params=_params(("arbitrary",), 48),
    )(y, bdc, bds)
    tl = min(512, l)
    return pl.pallas_call(
        functools.partial(_pos_dft_kernel, scale=(l * FOURIER_DIM) ** -0.5),
        name="pos_dft",
        grid=(l // tl, b),
        in_specs=[pl.BlockSpec((tl, 2 * l), lambda j, i: (j, 0)),
                  pl.BlockSpec((1, 2 * l, g), lambda j, i: (i, 0, 0))],
        out_specs=pl.BlockSpec((1, tl, g), lambda j, i: (i, j, 0)),
        out_shape=jax.ShapeDtypeStruct((b, l, g), bf16),
        compiler_params=_params(("arbitrary", "arbitrary"), 48),
    )(cs, zcs)


def _s5_matrices(a_re, a_im, b_re, b_im, c_re, c_im, log_dt):
    dt = jnp.exp(log_dt)[..., None]
    mag = jnp.exp(a_re * dt)
    abr = mag * jnp.cos(a_im * dt)
    abi = mag * jnp.sin(a_im * dt)
    den = a_re * a_re + a_im * a_im
    cr = ((abr - 1.0) * a_re + abi * a_im) / den
    ci = (abi * a_re - (abr - 1.0) * a_im) / den
    bbr = cr[..., None] * b_re - ci[..., None] * b_im
    bbi = cr[..., None] * b_im + ci[..., None] * b_re
    eye = jnp.eye(S5_GROUPS, dtype=f32)
    drive_re = jnp.einsum("dgph,gk->dghkp", bbr, eye).reshape(2, GROUP_W, S5_WIDTH)
    drive_im = jnp.einsum("dgph,gk->dghkp", bbi, eye).reshape(2, GROUP_W, S5_WIDTH)
    drive = jnp.concatenate([drive_re, drive_im], axis=-1).astype(bf16)
    read_re = jnp.einsum("dghp,gk->dgpkh", c_re, eye).reshape(2, S5_WIDTH, GROUP_W)
    read_im = jnp.einsum("dghp,gk->dgpkh", -c_im, eye).reshape(2, S5_WIDTH, GROUP_W)
    read = jnp.concatenate([read_re, read_im], axis=1).astype(bf16)
    a = jnp.stack([abr[0].reshape(-1), abi[0].reshape(-1), abr[1].reshape(-1), abi[1].reshape(-1)])
    return a, drive, read


def _s5_kernel(uf_ref, ub_ref, a_ref, drive_ref, read_ref, yf_ref, yb_ref, hf_ref, hb_ref, bf_ref, bb_ref,
               *, batch, chunk):
    s = S5_WIDTH

    @pl.when(pl.program_id(0) == 0)
    def _():
        hf_ref[...] = jnp.zeros_like(hf_ref)
        hb_ref[...] = jnp.zeros_like(hb_ref)

    bf_ref[...] = jnp.dot(uf_ref[...], drive_ref[0], preferred_element_type=f32)
    bb_ref[...] = jnp.dot(ub_ref[...], drive_ref[1], preferred_element_type=f32)

    def scan(buf, h_ref, ar, ai, reverse):
        ar = jnp.broadcast_to(ar, (batch, s))
        ai = jnp.broadcast_to(ai, (batch, s))

        def body(t, carry):
            hr, hi = carry
            step = (chunk - 1 - t) if reverse else t
            r0 = pl.multiple_of(step * batch, batch)
            nr = ar * hr - ai * hi + buf[pl.ds(r0, batch), 0:s]
            ni = ar * hi + ai * hr + buf[pl.ds(r0, batch), s:2 * s]
            buf[pl.ds(r0, batch), 0:s] = nr
            buf[pl.ds(r0, batch), s:2 * s] = ni
            return nr, ni

        hr, hi = lax.fori_loop(0, chunk, body, (h_ref[:, 0:s], h_ref[:, s:2 * s]))
        h_ref[:, 0:s] = hr
        h_ref[:, s:2 * s] = hi

    scan(bf_ref, hf_ref, a_ref[0:1, :], a_ref[1:2, :], False)
    scan(bb_ref, hb_ref, a_ref[2:3, :], a_ref[3:4, :], True)
    yf_ref[...] = jnp.dot(bf_ref[...].astype(bf16), read_ref[0], preferred_element_type=f32)
    yb_ref[...] = jnp.dot(bb_ref[...].astype(bf16), read_ref[1], preferred_element_type=f32)


def s5_scan(su_c, su_x, mats):
    a, drive, read = mats
    b, lc, g = su_c.shape
    l = su_x.shape[1]
    ch = S5_CHUNK
    nc, nx = lc // ch, l // ch
    lt = lc + l
    u = jnp.concatenate([su_c, su_x], axis=1).transpose(1, 0, 2).reshape(lt * b, g)
    rows = ch * b

    def bwd_idx(i):
        return (jnp.where(i < nc, nc - 1 - i, 2 * nc + nx - 1 - i), 0)

    yf, yb = pl.pallas_call(
        functools.partial(_s5_kernel, batch=b, chunk=ch),
        name="s5_scan",
        grid=(nc + nx,),
        in_specs=[pl.BlockSpec((rows, g), lambda i: (i, 0)),
                  pl.BlockSpec((rows, g), bwd_idx),
                  pl.BlockSpec(a.shape, lambda i: (0, 0)),
                  pl.BlockSpec(drive.shape, lambda i: (0, 0, 0)),
                  pl.BlockSpec(read.shape, lambda i: (0, 0, 0))],
        out_specs=[pl.BlockSpec((rows, g), lambda i: (i, 0)),
                   pl.BlockSpec((rows, g), bwd_idx)],
        out_shape=[jax.ShapeDtypeStruct((lt * b, g), f32)] * 2,
        scratch_shapes=[pltpu.VMEM((b, 2 * S5_WIDTH), f32), pltpu.VMEM((b, 2 * S5_WIDTH), f32),
                        pltpu.VMEM((rows, 2 * S5_WIDTH), f32), pltpu.VMEM((rows, 2 * S5_WIDTH), f32)],
        compiler_params=_params(("arbitrary",), 56),
    )(u, u, a, drive, read)
    yf = yf.reshape(lt, b, g).transpose(1, 0, 2)
    yb = yb.reshape(lt, b, g).transpose(1, 0, 2)
    return yf, yb


def _group_norm(p, g):
    return (p * lax.rsqrt(jnp.mean(p * p, axis=-1, keepdims=True) + EPS)) * g


def _merge_kernel(att_ref, conv_ref, four_ref, yf_ref, yb_ref, su_ref, x_ref, g1_ref, d_ref, wglu_ref,
                  gg_ref, wout_ref, o_ref):
    y = yf_ref[0] + yb_ref[0] + d_ref[...] * su_ref[0].astype(f32)
    gl = jax.nn.gelu(y)
    ssm = gl * jax.nn.sigmoid(jnp.dot(gl.astype(bf16), wglu_ref[...], preferred_element_type=f32))
    parts = [att_ref[0].astype(f32), conv_ref[0].astype(f32), four_ref[0].astype(f32), ssm]
    mix = None
    for k, p in enumerate(parts):
        pn = _group_norm(p, gg_ref[k:k + 1, :]).astype(bf16)
        t = jnp.dot(pn, wout_ref[k * GROUP_W:(k + 1) * GROUP_W, :], preferred_element_type=f32)
        mix = t if mix is None else mix + t
    o_ref[0] = x_ref[0] + g1_ref[0] * mix


def merge_groups(att, conv, four, yf, yb, y, x, g1, d_skip, w_glu, grp_g, w_out):
    b, l, d = x.shape
    g = GROUP_W
    tm = min(512, l)
    blk = lambda i, j: (i, j, 0)
    return pl.pallas_call(
        _merge_kernel,
        name="merge_out",
        grid=(b, l // tm),
        in_specs=[pl.BlockSpec((1, tm, g), blk), pl.BlockSpec((1, tm, g), blk), pl.BlockSpec((1, tm, g), blk),
                  pl.BlockSpec((1, tm, g), blk), pl.BlockSpec((1, tm, g), blk),
                  pl.BlockSpec((1, tm, g), lambda i, j: (i, j, 7)),
                  pl.BlockSpec((1, tm, d), blk),
                  pl.BlockSpec((1, 1, d), lambda i, j: (i, 0, 0)),
                  pl.BlockSpec((1, g), lambda i, j: (0, 0)),
                  pl.BlockSpec((g, g), lambda i, j: (0, 0)),
                  pl.BlockSpec((N_GROUPS, g), lambda i, j: (0, 0)),
                  pl.BlockSpec((d, d), lambda i, j: (0, 0))],
        out_specs=pl.BlockSpec((1, tm, d), blk),
        out_shape=jax.ShapeDtypeStruct((b, l, d), f32),
        compiler_params=_params(("arbitrary", "arbitrary"), 48),
    )(att, conv, four, yf, yb, y, x, g1.reshape(b, 1, d), d_skip.reshape(1, g), w_glu, grp_g, w_out)


def _peer_front_kernel(x_ref, g_ref, sc_ref, sh_ref, wqt_ref, keys_ref, ht_ref, st_ref):
    h = _norm_mod(x_ref[0], g_ref[...], sc_ref[0], sh_ref[0])
    ht = h.T.astype(bf16)
    ht_ref[...] = ht
    qt = jnp.dot(wqt_ref[...], ht, preferred_element_type=f32).astype(bf16)
    for hz in range(2 * PEER_HEADS):
        rows = slice(hz * PEER_HALF, (hz + 1) * PEER_HALF)
        st_ref[rows, :] = jnp.dot(keys_ref[hz], qt[rows, :], preferred_element_type=f32)


def peer_front(x, g, sc, sh, wq_t, keys):
    b, l, d = x.shape
    nq = wq_t.shape[0]
    tm = min(512, l)
    nt = l // tm
    return pl.pallas_call(
        _peer_front_kernel,
        name="peer_front",
        grid=(b, nt),
        in_specs=[pl.BlockSpec((1, tm, d), lambda i, j: (i, j, 0)),
                  pl.BlockSpec((1, d), lambda i, j: (0, 0)),
                  pl.BlockSpec((1, 1, d), lambda i, j: (i, 0, 0)),
                  pl.BlockSpec((1, 1, d), lambda i, j: (i, 0, 0)),
                  pl.BlockSpec((nq, d), lambda i, j: (0, 0)),
                  pl.BlockSpec(keys.shape, lambda i, j: (0, 0, 0))],
        out_specs=[pl.BlockSpec((d, tm), lambda i, j: (0, i * nt + j)),
                   pl.BlockSpec((nq, tm), lambda i, j: (0, i * nt + j))],
        out_shape=[jax.ShapeDtypeStruct((d, b * l), bf16), jax.ShapeDtypeStruct((nq, b * l), f32)],
        compiler_params=_params(("arbitrary", "arbitrary"), 48),
    )(x, g.reshape(1, d), sc.reshape(b, 1, d), sh.reshape(b, 1, d), wq_t, keys)


_CAND = [(j, k) for j in range(PEER_TOPK) for k in range(PEER_TOPK) if (j + 1) * (k + 1) <= PEER_TOPK]
BIG = 3e38


def _round_bf16(x):
    return x.astype(bf16).astype(f32)


def _dup_bf16_bits(x):
    bits = lax.bitcast_convert_type(x, jnp.uint32)
    return bits | (bits >> 16)


def _peer_select_kernel(st_ref, p1_ref, p2_ref, c_ref, ta_scr, tb_scr, cand_scr, *, tl):
    nk = PEER_TOPK
    big = jnp.full((PEER_HEADS, LANES), BIG, f32)

    def lane_group(lg, carry):
        lanes = pl.ds(pl.multiple_of(lg * LANES, LANES), LANES)

        def extract(k, prev):
            nxt = []
            for z in range(2):
                ms = []
                for h in range(PEER_HEADS):
                    r0 = (2 * h + z) * PEER_HALF
                    sc = st_ref[r0:r0 + PEER_HALF, lanes]
                    ms.append(jnp.max(jnp.where(sc < prev[z][h:h + 1], sc, NEG), axis=0, keepdims=True))
                nxt.append(jnp.concatenate(ms, axis=0))
            ta_scr[k] = nxt[0]
            tb_scr[k] = nxt[1]
            return tuple(nxt)

        lax.fori_loop(0, nk, extract, (big, big))

        for idx, (j, k) in enumerate(_CAND):
            cand_scr[idx] = ta_scr[j] + tb_scr[k]

        def nth(k, prev):
            return functools.reduce(jnp.maximum, [jnp.where(cand_scr[i] < prev, cand_scr[i], NEG)
                                                  for i in range(len(_CAND))])

        v16 = lax.fori_loop(0, nk, nth, big)
        a0 = ta_scr[0]
        b0 = tb_scr[0]
        top = a0 + b0
        keep = [cand_scr[i] >= v16 for i in range(len(_CAND))]
        z = functools.reduce(jnp.add, [jnp.where(keep[i], jnp.exp(cand_scr[i] - top), 0.0)
                                       for i in range(len(_CAND))])
        inv = 1.0 / z
        pa = [_round_bf16(jnp.exp(ta_scr[j] - a0) * inv) for j in range(nk)]
        pb = [_round_bf16(jnp.exp(tb_scr[k] - b0)) for k in range(nk)]
        cut = functools.reduce(jnp.minimum, [jnp.where(keep[i], _round_bf16(pa[j] * pb[k]), BIG)
                                             for i, (j, k) in enumerate(_CAND)])
        c_ref[:, lanes] = _dup_bf16_bits(cut)
        for h in range(PEER_HEADS):
            r0 = 2 * h * PEER_HALF
            p1 = _round_bf16(jnp.exp(st_ref[r0:r0 + PEER_HALF, lanes] - a0[h:h + 1]) * inv[h:h + 1])
            p1_ref[h, :, lanes] = _dup_bf16_bits(p1)
            p2_ref[h, :, lanes] = jnp.exp(st_ref[r0 + PEER_HALF:r0 + 2 * PEER_HALF, lanes]
                                          - b0[h:h + 1]).astype(bf16)
        return carry

    lax.fori_loop(0, tl // LANES, lane_group, 0)


def peer_select(st):
    nq, t = st.shape
    tl = min(512, t)
    return pl.pallas_call(
        functools.partial(_peer_select_kernel, tl=tl),
        name="peer_select",
        grid=(t // tl,),
        in_specs=[pl.BlockSpec((nq, tl), lambda i: (0, i))],
        out_specs=[pl.BlockSpec((PEER_HEADS, PEER_KEYS, tl), lambda i: (0, 0, i)),
                   pl.BlockSpec((PEER_HEADS, PEER_KEYS, tl), lambda i: (0, 0, i)),
                   pl.BlockSpec((PEER_HEADS, tl), lambda i: (0, i))],
        out_shape=[jax.ShapeDtypeStruct((PEER_HEADS, PEER_KEYS, t), jnp.uint32),
                   jax.ShapeDtypeStruct((PEER_HEADS, PEER_KEYS, t), bf16),
                   jax.ShapeDtypeStruct((PEER_HEADS, t), jnp.uint32)],
        scratch_shapes=[pltpu.VMEM((PEER_TOPK, PEER_HEADS, LANES), f32),
                        pltpu.VMEM((PEER_TOPK, PEER_HEADS, LANES), f32),
                        pltpu.VMEM((len(_CAND), PEER_HEADS, LANES), f32)],
        compiler_params=_params(("arbitrary",), 48),
    )(st)


def _row_tile_bf16(row):
    return pltpu.bitcast(jnp.broadcast_to(row, (SUBLANES, LANES)), bf16)


def _peer_dense_kernel(ht_ref, u_ref, vt_ref, p1_ref, p2_ref, c_ref, x_ref, g2_ref, o_ref,
                       st_scr, a_scr, acc_scr, *, n_i1, tm, ne):
    s = pl.program_id(0)
    j3 = lax.rem(jnp.maximum(s - 2, 0), ne)

    @pl.when(s == 0)
    def _():
        st_scr[...] = jnp.zeros_like(st_scr)
        a_scr[...] = jnp.zeros_like(a_scr)

    @pl.when(j3 == 0)
    def _():
        acc_scr[...] = jnp.zeros_like(acc_scr)

    def stages(cur, prv):
        d = acc_scr.shape[0]
        te = n_i1 * PEER_KEYS
        for nt in range(tm // PEER_SUB):
          cols = slice(nt * PEER_SUB, (nt + 1) * PEER_SUB)
          for piece in range(PEER_PIECES):
            orow = slice(piece * d // PEER_PIECES, (piece + 1) * d // PEER_PIECES)
            acc_scr[orow, cols] += jnp.dot(vt_ref[0, orow, :], a_scr[prv, :, cols], preferred_element_type=f32)
            for i1 in range(piece * n_i1 // PEER_PIECES, (piece + 1) * n_i1 // PEER_PIECES):
                for lg in range(PEER_SUB // LANES):
                    lanes = slice(nt * PEER_SUB + lg * LANES, nt * PEER_SUB + (lg + 1) * LANES)
                    for r0 in range(0, PEER_KEYS // BF16_ROWS, PEER_GATE_TILES):
                        gate = [None] * PEER_GATE_TILES
                        for h in range(PEER_HEADS):
                            p1 = _row_tile_bf16(p1_ref[h, i1:i1 + 1, lanes])
                            cut = _row_tile_bf16(c_ref[h:h + 1, lanes])
                            for r in range(PEER_GATE_TILES):
                                w = p1 * p2_ref[h, (r0 + r) * BF16_ROWS:(r0 + r + 1) * BF16_ROWS, lanes]
                                sel = jnp.where(w >= cut, w, jnp.zeros_like(w))
                                gate[r] = sel if gate[r] is None else gate[r] + sel
                        for r in range(PEER_GATE_TILES):
                            rows = slice(i1 * PEER_KEYS + (r0 + r) * BF16_ROWS,
                                         i1 * PEER_KEYS + (r0 + r + 1) * BF16_ROWS)
                            a_scr[cur, rows, lanes] = (jax.nn.gelu(st_scr[prv, rows, lanes].astype(bf16))
                                                       * gate[r])
            erow = slice(piece * te // PEER_PIECES, (piece + 1) * te // PEER_PIECES)
            st_scr[cur, erow, cols] = jnp.dot(u_ref[erow, :], ht_ref[:, cols], preferred_element_type=f32)

    parity = lax.rem(s, 2)

    @pl.when(parity == 0)
    def _():
        stages(0, 1)

    @pl.when(parity == 1)
    def _():
        stages(1, 0)

    @pl.when((s >= 2) & (j3 == ne - 1))
    def _():
        o_ref[...] = x_ref[...] + g2_ref[0] * acc_scr[...].T


def peer_dense(ht, u, vt, p1, p2, c, x2d, g2, tokens_per_batch):
    d, t = ht.shape
    e = u.shape[0]
    tm = min(512, tokens_per_batch)
    n_i1 = PEER_CHUNK // PEER_KEYS
    te = PEER_CHUNK
    ne = e // te
    nb = tokens_per_batch // tm
    n = (t // tm) * ne
    bsz = g2.shape[0]

    def stage(lag):
        def split(s):
            q = jnp.clip(s - lag, 0, n - 1)
            return q // ne, lax.rem(q, ne)
        return split

    s1, s2, s3 = stage(0), stage(1), stage(2)
    return pl.pallas_call(
        functools.partial(_peer_dense_kernel, n_i1=n_i1, tm=tm, ne=ne),
        name="peer_dense",
        grid=(n + 2,),
        in_specs=[pl.BlockSpec((d, tm), lambda s: (0, s1(s)[0])),
                  pl.BlockSpec((te, d), lambda s: (s1(s)[1], 0)),
                  pl.BlockSpec((1, d, te), lambda s: (s3(s)[1], 0, 0)),
                  pl.BlockSpec((PEER_HEADS, n_i1, tm), lambda s: (0, s2(s)[1], s2(s)[0])),
                  pl.BlockSpec((PEER_HEADS, PEER_KEYS, tm), lambda s: (0, 0, s2(s)[0])),
                  pl.BlockSpec((PEER_HEADS, tm), lambda s: (0, s2(s)[0])),
                  pl.BlockSpec((tm, d), lambda s: (s3(s)[0], 0)),
                  pl.BlockSpec((1, 1, d), lambda s: (s3(s)[0] // nb, 0, 0))],
        out_specs=pl.BlockSpec((tm, d), lambda s: (s3(s)[0], 0)),
        out_shape=jax.ShapeDtypeStruct((t, d), f32),
        scratch_shapes=[pltpu.VMEM((2, te, tm), f32), pltpu.VMEM((2, te, tm), bf16), pltpu.VMEM((d, tm), f32)],
        compiler_params=_params(("arbitrary",), 56),
    )(ht, u, vt, p1, p2, c, x2d, g2.reshape(bsz, 1, d))


def peer_ffn_residual(x, g, sc, sh, g2, wq_t, keys, u, vt):
    b, l, d = x.shape
    ht, st = peer_front(x, g, sc, sh, wq_t, keys)
    p1, p2, c = peer_select(st)
    out = peer_dense(ht, u, vt, p1, p2, c, x.reshape(b * l, d), g2, l)
    return out.reshape(b, l, d)


def _final_norm_kernel(x_ref, g_ref, o_ref):
    x = x_ref[...]
    o_ref[...] = (x * lax.rsqrt(jnp.mean(x * x, axis=-1, keepdims=True) + EPS)) * g_ref[...]


def final_norm(x2d, g):
    t, d = x2d.shape
    tm = min(1024, t)
    return pl.pallas_call(
        _final_norm_kernel,
        grid=(t // tm,),
        in_specs=[pl.BlockSpec((tm, d), lambda i: (i, 0)),
                  pl.BlockSpec((1, d), lambda i: (0, 0))],
        out_specs=pl.BlockSpec((tm, d), lambda i: (i, 0)),
        out_shape=jax.ShapeDtypeStruct((t, d), x2d.dtype),
        compiler_params=_params(("arbitrary",)),
    )(x2d, g.reshape(1, d))


def kernel(x, c, ctx, c_ctx, ada_w, ada_b, norm1_g, norm2_g, w_in, na_rpb, conv_w, s5_a_re, s5_a_im, s5_b_re, s5_b_im, s5_c_re, s5_c_im, s5_log_dt, s5_d, s5_w_glu, grp_g, w_out, peer_wq, peer_keys, peer_u, peer_v, final_g):
    b, l, d = x.shape
    lc = ctx.shape[1]
    depth = ada_w.shape[0]
    g = GROUP_W

    n_rows = -(-(b + 1) // 8) * 8
    cc = jnp.zeros((n_rows, d), f32).at[:b].set(c).at[b].set(c_ctx)
    mods = ada_modulation(cc, ada_w, ada_b)

    dft_x = _dft_tables(l)
    dft_c = _dft_tables(lc)

    for i in range(depth):
        update_ctx = i < depth - 1
        sh1, sc1, g1, sh2, sc2, g2 = jnp.split(mods[i, :b], N_MOD, axis=-1)
        csh1, csc1, cg1, csh2, csc2, cg2 = [jnp.broadcast_to(m, (b, d))
                                            for m in jnp.split(mods[i, b:b + 1], N_MOD, axis=-1)]
        w_in_i = w_in[i].astype(bf16)
        w_out_i = w_out[i].astype(bf16)
        w_glu_i = s5_w_glu[i].astype(bf16)
        wq_t = peer_wq[i].T.astype(bf16)
        keys = peer_keys[i].reshape(2 * PEER_HEADS, PEER_KEYS, PEER_HALF).astype(bf16)
        u_i = peer_u[i].astype(bf16)
        vt_i = peer_v[i].reshape(-1, PEER_CHUNK, d).transpose(0, 2, 1).astype(bf16)
        bias = _na_bias_tables(na_rpb[i], l // GRID_W)
        s5m = _s5_matrices(s5_a_re[i], s5_a_im[i], s5_b_re[i], s5_b_im[i], s5_c_re[i], s5_c_im[i], s5_log_dt[i])

        y = norm_mm(x, norm1_g[i], sc1, sh1, w_in_i)
        yc = norm_mm(ctx, norm1_g[i], csc1, csh1, w_in_i)

        att_x = neighbourhood_attention(y, yc, bias)
        conv_x = gated_conv(y, conv_w[i])
        four_x = fourier_mix(y, dft_x)
        yf, yb = s5_scan(yc[:, :, 7 * g:], y[:, :, 7 * g:], s5m)

        x = merge_groups(att_x, conv_x, four_x, yf[:, lc:], yb[:, lc:], y, x, g1,
                         s5_d[i], w_glu_i, grp_g[i], w_out_i)
        x = peer_ffn_residual(x, norm2_g[i], sc2, sh2, g2, wq_t, keys, u_i, vt_i)

        if update_ctx:
            att_c = ctx_attention(yc)
            conv_c = gated_conv(yc, conv_w[i])
            four_c = fourier_mix(yc, dft_c)
            ctx = merge_groups(att_c, conv_c, four_c, yf[:, :lc], yb[:, :lc], yc, ctx, cg1,
                               s5_d[i], w_glu_i, grp_g[i], w_out_i)
            ctx = peer_ffn_residual(ctx, norm2_g[i], csc2, csh2, cg2, wq_t, keys, u_i, vt_i)

    return final_norm(x.reshape(b * l, d), final_g).reshape(b, l, d)
```
